```python
import math
import jax
import jax.numpy as jnp
from jax import lax
import numpy as np


D_MODEL = 1024
BATCH = 8
SEQ = 2048
DEPTH = 4

CTX_LEN = 256
GRID_W = 64
HEAD_DIM = 64
ROPE_THETA = 10000.0
EPS = 1e-6
NEG_INF = -1e30
D_FF = 2816
N_BRANCH = 4
BRANCH_W = 256
WA_Q_HEADS = 4
WA_KV_HEADS = 2
WA_WINDOW = 128
WA_BLOCK = 128
NA_HEADS = 4
NA_ROWS = 8
NA_COLS = 16
NA_QCOLS = 16
NA_KCOLS = 32
GA_Q_HEADS = 4
GA_KV_HEADS = 2
GA_BLOCK = 128
SSM_HEADS = 4
SSM_HEAD_DIM = 64
SSM_GROUPS = 2
SSM_STATE = 128
SSM_CONV = 5
SSM_CHUNK = 128
SSM_INNER = SSM_HEADS * SSM_HEAD_DIM
SSM_CONV_CH = SSM_INNER + 2 * SSM_GROUPS * SSM_STATE
IN_SIZES = (WA_Q_HEADS * HEAD_DIM, WA_KV_HEADS * HEAD_DIM, WA_KV_HEADS * HEAD_DIM,
            NA_HEADS * HEAD_DIM, NA_HEADS * HEAD_DIM, NA_HEADS * HEAD_DIM,
            GA_Q_HEADS * HEAD_DIM, GA_KV_HEADS * HEAD_DIM, GA_KV_HEADS * HEAD_DIM,
            SSM_INNER, SSM_CONV_CH, 2 * SSM_HEADS,
            N_BRANCH * D_MODEL)

kernel_name = 'hybrid_flow_backbone'


def rms_norm(x, g):
    xf = x.astype(jnp.float32)
    y = xf * lax.rsqrt(jnp.mean(xf * xf, axis=-1, keepdims=True) + EPS)
    return (y * g.astype(jnp.float32)).astype(x.dtype)


def swiglu(h, wg, wu, wd):
    return (jax.nn.silu(h @ wg) * (h @ wu)) @ wd


def split_heads(t):
    return t.reshape(t.shape[:-1] + (-1, HEAD_DIM))


def rope_tables(seq):
    pos = jnp.arange(seq)
    axes = jnp.stack([pos // GRID_W, pos % GRID_W], axis=-1).astype(jnp.float32)
    quarter = HEAD_DIM // 4
    inv = 1.0 / (ROPE_THETA ** (jnp.arange(quarter, dtype=jnp.float32) * 4.0 / HEAD_DIM))
    ang = axes[:, :, None] * inv
    return jnp.cos(ang), jnp.sin(ang)


def apply_rope(x, cos, sin):
    shp = x.shape
    xs = x.reshape(shp[:-1] + (2, 2, HEAD_DIM // 4))
    x1, x2 = xs[..., 0, :], xs[..., 1, :]
    cs = cos[:, None].astype(x.dtype)
    sn = sin[:, None].astype(x.dtype)
    out = jnp.stack([x1 * cs - x2 * sn, x2 * cs + x1 * sn], axis=-2)
    return out.reshape(shp)


def joint_softmax(parts):
    sizes = [p.shape[-1] for p in parts]
    p = jax.nn.softmax(jnp.concatenate(parts, axis=-1), axis=-1)
    return jnp.split(p, np.cumsum(sizes)[:-1].tolist(), axis=-1)


def context_attention(q, k, v, sink):
    bsz, lq, hq, d = q.shape
    hk = k.shape[2]
    g = hq // hk
    qg = q.reshape(bsz, lq, hk, g, d)
    s = jnp.einsum('bqhgd,bkhd->bhgqk', qg, k).astype(jnp.float32) * d ** -0.5
    if sink is not None:
        s_sink = jnp.broadcast_to(sink.astype(jnp.float32).reshape(1, hk, g, 1, 1), s.shape[:-1] + (1,))
        p, _ = joint_softmax([s, s_sink])
    else:
        p = jax.nn.softmax(s, axis=-1)
    o = jnp.einsum('bhgqk,bkhd->bqhgd', p.astype(v.dtype), v)
    return o.reshape(bsz, lq, hq * d)


def window_attention(q, k, v, kc, vc, sink):
    bsz, seq, hq, d = q.shape
    hk = k.shape[2]
    g = hq // hk
    nb = seq // WA_BLOCK
    scale = d ** -0.5
    qb = q.reshape(bsz, nb, WA_BLOCK, hk, g, d)

    def band(t):
        tp = jnp.pad(t, ((0, 0), (WA_WINDOW, WA_WINDOW), (0, 0), (0, 0))).reshape(bsz, nb + 2, WA_BLOCK, hk, d)
        return jnp.concatenate([tp[:, :-2], tp[:, 1:-1], tp[:, 2:]], axis=2)

    kb, vb = band(k), band(v)
    s_loc = jnp.einsum('bnqhgd,bnkhd->bnhgqk', qb, kb).astype(jnp.float32) * scale
    blk = jnp.arange(nb)[:, None, None] * WA_BLOCK
    qpos = blk + jnp.arange(WA_BLOCK)[None, :, None]
    kpos = blk - WA_WINDOW + jnp.arange(3 * WA_BLOCK)[None, None, :]
    valid = (jnp.abs(qpos - kpos) <= WA_WINDOW) & (kpos >= 0) & (kpos < seq)
    s_loc = jnp.where(valid[None, :, None, None], s_loc, NEG_INF)
    s_ctx = jnp.einsum('bnqhgd,bchd->bnhgqc', qb, kc).astype(jnp.float32) * scale
    s_sink = jnp.broadcast_to(sink.astype(jnp.float32).reshape(1, 1, hk, g, 1, 1), s_ctx.shape[:-1] + (1,))
    p_loc, p_ctx, _ = joint_softmax([s_loc, s_ctx, s_sink])
    o = (jnp.einsum('bnhgqk,bnkhd->bnqhgd', p_loc.astype(vb.dtype), vb)
         + jnp.einsum('bnhgqc,bchd->bnqhgd', p_ctx.astype(vc.dtype), vc))
    return o.reshape(bsz, seq, hq * d)


def neighborhood_attention(q, k, v, kc, vc, rpb):
    bsz, seq, h, d = q.shape
    rows = seq // GRID_W
    rw = min(NA_ROWS, rows)
    ncb = GRID_W // NA_QCOLS
    scale = d ** -0.5
    r = jnp.arange(rows)
    row_start = jnp.clip(r - rw // 2, 0, rows - rw)
    idx_r = row_start[:, None] + jnp.arange(rw)[None, :]
    qcol = jnp.arange(ncb)[:, None] * NA_QCOLS + jnp.arange(NA_QCOLS)[None, :]
    qcol_start = jnp.clip(qcol - NA_COLS // 2, 0, GRID_W - NA_COLS)
    kcol_start = jnp.clip(jnp.arange(ncb) * NA_QCOLS - NA_COLS // 2, 0, GRID_W - NA_KCOLS)
    idx_c = kcol_start[:, None] + jnp.arange(NA_KCOLS)[None, :]
    nk = rw * NA_KCOLS

    def gather(t):
        tg = t.reshape(bsz, rows, GRID_W, h, d)
        return tg[:, idx_r[:, None, :, None], idx_c[None, :, None, :]].reshape(bsz, rows, ncb, nk, h, d)

    kn, vn = gather(k), gather(v)
    qg = q.reshape(bsz, rows, ncb, NA_QCOLS, h, d)
    col_ok = (idx_c[:, None, :] >= qcol_start[:, :, None]) & (idx_c[:, None, :] < qcol_start[:, :, None] + NA_COLS)
    col_ok = jnp.broadcast_to(col_ok[:, :, None, :], (ncb, NA_QCOLS, rw, NA_KCOLS)).reshape(ncb, NA_QCOLS, nk)
    dr = idx_r - r[:, None] + NA_ROWS - 1
    dc = jnp.clip(idx_c[:, None, :] - qcol[:, :, None] + NA_COLS - 1, 0, 2 * NA_COLS - 2)
    bias = rpb[:, dr[:, None, None, :, None], dc[None, :, :, None, :]]
    bias = bias.reshape(h, rows, ncb, NA_QCOLS, nk).transpose(1, 2, 0, 3, 4).astype(jnp.float32)
    s_nb = jnp.einsum('brjqhd,brjkhd->brjhqk', qg, kn).astype(jnp.float32) * scale + bias
    s_nb = jnp.where(col_ok[None, None, :, None], s_nb, NEG_INF)
    s_ctx = jnp.einsum('brjqhd,bchd->brjhqc', qg, kc).astype(jnp.float32) * scale
    p_nb, p_ctx = joint_softmax([s_nb, s_ctx])
    o = (jnp.einsum('brjhqk,brjkhd->brjqhd', p_nb.astype(vn.dtype), vn)
         + jnp.einsum('brjhqc,bchd->brjqhd', p_ctx.astype(vc.dtype), vc))
    return o.reshape(bsz, seq, h * d)


def dense_block_attention(q, k, v, kc, vc):
    bsz, seq, hq, d = q.shape
    hk = k.shape[2]
    g = hq // hk
    nb = seq // GA_BLOCK
    scale = d ** -0.5
    qb = q.reshape(bsz, nb, GA_BLOCK, hk, g, d).transpose(1, 0, 2, 3, 4, 5)

    def one_block(qi):
        s_lat = jnp.einsum('bqhgd,bkhd->bhgqk', qi, k).astype(jnp.float32) * scale
        s_ctx = jnp.einsum('bqhgd,bkhd->bhgqk', qi, kc).astype(jnp.float32) * scale
        p_lat, p_ctx = joint_softmax([s_lat, s_ctx])
        return (jnp.einsum('bhgqk,bkhd->bqhgd', p_lat.astype(v.dtype), v)
                + jnp.einsum('bhgqk,bkhd->bqhgd', p_ctx.astype(vc.dtype), vc))

    o = lax.map(one_block, qb)
    return o.transpose(1, 0, 2, 3, 4, 5).reshape(bsz, seq, hq * d)


def segsum(a):
    t = a.shape[-1]
    a_rep = jnp.broadcast_to(a[..., :, None], a.shape + (t,))
    strict = jnp.tril(jnp.ones((t, t), dtype=bool), -1)
    cs = jnp.cumsum(jnp.where(strict, a_rep, 0.0), axis=-2)
    return jnp.where(jnp.tril(jnp.ones((t, t), dtype=bool)), cs, -jnp.inf)


def ssd(x, dt, a, bm, cm, init):
    bsz, seq, h, p = x.shape
    g, n = bm.shape[2], bm.shape[3]
    nc = seq // SSM_CHUNK
    rep = h // g
    f32 = jnp.float32
    bh = jnp.repeat(bm.astype(f32), rep, axis=2).reshape(bsz, nc, SSM_CHUNK, h, n)
    ch = jnp.repeat(cm.astype(f32), rep, axis=2).reshape(bsz, nc, SSM_CHUNK, h, n)
    xdt = (x.astype(f32) * dt[..., None]).reshape(bsz, nc, SSM_CHUNK, h, p)
    da = (dt * a).reshape(bsz, nc, SSM_CHUNK, h).transpose(0, 3, 1, 2)
    da_cum = jnp.cumsum(da, axis=-1)
    decay = jnp.exp(segsum(da))
    scores = jnp.einsum('bclhn,bcshn->bhcls', ch, bh) * decay
    y_diag = jnp.einsum('bhcls,bcshp->bclhp', scores, xdt)
    decay_states = jnp.exp(da_cum[..., -1:] - da_cum)
    states = jnp.einsum('bclhn,bhcl,bclhp->bchpn', bh, decay_states, xdt)
    chunk_decay = jnp.exp(da_cum[..., -1])

    def step(s, inp):
        st, dec = inp
        return s * dec[..., None, None] + st, s

    final, s_in = lax.scan(step, init, (states.transpose(1, 0, 2, 3, 4), chunk_decay.transpose(2, 0, 1)))
    s_in = s_in.transpose(1, 0, 2, 3, 4)
    y_off = jnp.einsum('bclhn,bchpn,bhcl->bclhp', ch, s_in, jnp.exp(da_cum))
    return (y_diag + y_off).reshape(bsz, seq, h, p), final


def depthwise_conv(u, w, b):
    pad = SSM_CONV // 2
    y = lax.conv_general_dilated(u, w.astype(u.dtype)[:, None, :], (1,), [(pad, pad)],
                                 dimension_numbers=('NWC', 'WIO', 'NWC'), feature_group_count=u.shape[-1])
    return y + b.astype(u.dtype)


def ssm_mixer(z, xbc, dt_raw, zc, xbcc, dtc_raw, lp, with_ctx):
    def prep(xbc_, dt_raw_):
        u = jax.nn.silu(depthwise_conv(xbc_, lp['ssm_conv_w'], lp['ssm_conv_b']))
        xs, bm, cm = jnp.split(u, [SSM_INNER, SSM_INNER + SSM_GROUPS * SSM_STATE], axis=-1)
        xs = xs.reshape(xs.shape[:-1] + (SSM_HEADS, SSM_HEAD_DIM))
        bm = bm.reshape(bm.shape[:-1] + (SSM_GROUPS, SSM_STATE))
        cm = cm.reshape(cm.shape[:-1] + (SSM_GROUPS, SSM_STATE))
        dtr = dt_raw_.astype(jnp.float32).reshape(dt_raw_.shape[:-1] + (2, SSM_HEADS))
        dt = jax.nn.softplus(dtr + lp['ssm_dt_bias'].astype(jnp.float32))
        return xs, bm, cm, dt

    a = -jnp.exp(lp['ssm_a_log'].astype(jnp.float32))
    xl, bl, cl, dtl = prep(xbc, dt_raw)
    xc, bc, cc, dtc = prep(xbcc, dtc_raw)
    init = jnp.zeros((xl.shape[0], SSM_HEADS, SSM_HEAD_DIM, SSM_STATE), jnp.float32)
    flip = lambda t: jnp.flip(t, axis=1)
    yc_f, sc_f = ssd(xc, dtc[..., 0, :], a[0], bc, cc, init)
    yl_f, _ = ssd(xl, dtl[..., 0, :], a[0], bl, cl, sc_f)
    yc_b, sc_b = ssd(flip(xc), flip(dtc[..., 1, :]), a[1], flip(bc), flip(cc), init)
    yl_b, _ = ssd(flip(xl), flip(dtl[..., 1, :]), a[1], flip(bl), flip(cl), sc_b)
    dskip = lp['ssm_d'].astype(jnp.float32)[:, None]

    def finish(yf, yb_rev, xs, z_):
        y = yf + flip(yb_rev) + dskip * xs.astype(jnp.float32)
        y = y.reshape(y.shape[:2] + (SSM_INNER,)).astype(z_.dtype) * jax.nn.silu(z_)
        return rms_norm(y, lp['ssm_norm'])

    y_lat = finish(yl_f, yl_b, xl, z)
    y_ctx = finish(yc_f, yc_b, xc, zc) if with_ctx else None
    return y_lat, y_ctx


def merge(ys, gate_pre, lp):
    gates = jax.nn.sigmoid(gate_pre + lp['b_gate']).reshape(gate_pre.shape[:-1] + (N_BRANCH, D_MODEL))
    yb = jnp.stack(ys, axis=-2)
    branch = jnp.einsum('...kw,kwd->...kd', yb, lp['w_branch'])
    return jnp.sum(gates * branch, axis=-2) @ lp['w_out']


def token_mix(h, hc, lp, cos, sin, with_ctx):
    offs = np.cumsum(IN_SIZES)[:-1].tolist()
    aq, ak, av, bq, bk, bv, cq, ck, cv, z, xbc, dt, gl = jnp.split(h @ lp['w_in'], offs, axis=-1)
    aqc, akc, avc, bqc, bkc, bvc, cqc, ckc, cvc, zc, xbcc, dtc, gc = jnp.split(hc @ lp['w_in'], offs, axis=-1)
    hs = split_heads
    rope = lambda t: apply_rope(t, cos, sin)
    qn = lambda t: rms_norm(hs(t), lp['qk_norm_q'])
    kn = lambda t: rms_norm(hs(t), lp['qk_norm_k'])
    akc_h, avc_h = hs(akc), hs(avc)
    bkc_h, bvc_h = hs(bkc), hs(bvc)
    ckc_h, cvc_h = kn(ckc), hs(cvc)
    ya = window_attention(rope(hs(aq)), rope(hs(ak)), hs(av), akc_h, avc_h, lp['attn_sink'])
    yb = neighborhood_attention(hs(bq), hs(bk), hs(bv), bkc_h, bvc_h, lp['na_rpb'])
    yc = dense_block_attention(rope(qn(cq)), rope(kn(ck)), hs(cv), ckc_h, cvc_h)
    yd, ydc = ssm_mixer(z, xbc, dt, zc, xbcc, dtc, lp, with_ctx)
    y_lat = merge([ya, yb, yc, yd], gl, lp)
    if not with_ctx:
        return y_lat, None
    yac = context_attention(hs(aqc), akc_h, avc_h, lp['attn_sink'])
    ybc = context_attention(hs(bqc), bkc_h, bvc_h, None)
    ycc = context_attention(qn(cqc), ckc_h, cvc_h, None)
    y_ctx = merge([yac, ybc, ycc, ydc], gc, lp)
    return y_lat, y_ctx


def layer(x, xc, c_act, cc_act, lp, cos, sin, with_ctx):
    mod = (c_act @ lp['w_mod'] + lp['b_mod']).reshape(c_act.shape[0], 1, 9, D_MODEL)
    modc = (cc_act @ lp['w_mod'] + lp['b_mod']).reshape(9, D_MODEL)
    m = lambda i: mod[:, :, i]
    mc = lambda i: modc[i]
    ada = lambda t, g, shift, scale: rms_norm(t, g) * (1 + scale) + shift
    ffn = lambda t, pre: swiglu(t, lp[pre + '_w_gate'], lp[pre + '_w_up'], lp[pre + '_w_down'])
    x = x + 0.5 * m(2) * ffn(ada(x, lp['norm_ffn1'], m(0), m(1)), 'ffn1')
    xc = xc + 0.5 * mc(2) * ffn(ada(xc, lp['norm_ffn1'], mc(0), mc(1)), 'ffn1')
    y, yc = token_mix(ada(x, lp['norm_mix'], m(3), m(4)), ada(xc, lp['norm_mix'], mc(3), mc(4)),
                      lp, cos, sin, with_ctx)
    x = x + m(5) * y
    x = x + 0.5 * m(8) * ffn(ada(x, lp['norm_ffn2'], m(6), m(7)), 'ffn2')
    if with_ctx:
        xc = xc + mc(5) * yc
        xc = xc + 0.5 * mc(8) * ffn(ada(xc, lp['norm_ffn2'], mc(6), mc(7)), 'ffn2')
    return x, xc


def setup_inputs(seed: int = 0) -> dict:
    key = jax.random.key(seed)
    k = jax.random.split(key, 32)
    f32 = jnp.float32
    D = D_MODEL
    L = DEPTH
    n_in = sum(IN_SIZES)
    nrm = lambda kk, shape, scale: jax.random.normal(kk, shape, f32) * scale
    gain = lambda kk, shape: 1.0 + 0.05 * jax.random.normal(kk, shape, f32)
    dt0 = jnp.exp(jax.random.uniform(k[19], (L, 2, SSM_HEADS), f32, math.log(1e-3), math.log(1e-1)))
    return {
        'x': nrm(k[0], (BATCH, SEQ, D), 1.0),
        'c': nrm(k[1], (BATCH, D), 1.0),
        'ctx': nrm(k[2], (BATCH, CTX_LEN, D), 1.0),
        'c_ctx': nrm(k[3], (D,), 1.0),
        'w_mod': nrm(k[4], (L, D, 9 * D), 0.5 * D ** -0.5),
        'b_mod': nrm(k[5], (L, 9 * D), 0.02),
        'norm_ffn1': gain(k[6], (L, D)),
        'ffn1_w_gate': nrm(k[7], (L, D, D_FF), D ** -0.5),
        'ffn1_w_up': nrm(k[8], (L, D, D_FF), D ** -0.5),
        'ffn1_w_down': nrm(k[9], (L, D_FF, D), D_FF ** -0.5),
        'norm_mix': gain(k[10], (L, D)),
        'w_in': nrm(k[11], (L, D, n_in), D ** -0.5),
        'b_gate': nrm(k[12], (L, N_BRANCH * D), 0.02),
        'attn_sink': nrm(k[13], (L, WA_Q_HEADS), 0.5),
        'na_rpb': nrm(k[14], (L, NA_HEADS, 2 * NA_ROWS - 1, 2 * NA_COLS - 1), 0.1),
        'qk_norm_q': gain(k[15], (L, HEAD_DIM)),
        'qk_norm_k': gain(k[16], (L, HEAD_DIM)),
        'ssm_conv_w': nrm(k[17], (L, SSM_CONV, SSM_CONV_CH), SSM_CONV ** -0.5),
        'ssm_conv_b': nrm(k[18], (L, SSM_CONV_CH), 0.02),
        'ssm_dt_bias': dt0 + jnp.log(-jnp.expm1(-dt0)),
        'ssm_a_log': jnp.log(jax.random.uniform(k[20], (L, 2, SSM_HEADS), f32, 1.0, 16.0)),
        'ssm_d': 1.0 + nrm(k[21], (L, SSM_HEADS), 0.1),
        'ssm_norm': gain(k[22], (L, SSM_INNER)),
        'w_branch': nrm(k[23], (L, N_BRANCH, BRANCH_W, D), BRANCH_W ** -0.5),
        'w_out': nrm(k[24], (L, D, D), D ** -0.5),
        'norm_ffn2': gain(k[25], (L, D)),
        'ffn2_w_gate': nrm(k[26], (L, D, D_FF), D ** -0.5),
        'ffn2_w_up': nrm(k[27], (L, D, D_FF), D ** -0.5),
        'ffn2_w_down': nrm(k[28], (L, D_FF, D), D_FF ** -0.5),
        'final_norm': gain(k[29], (D,)),
    }


def reference(x, c, ctx, c_ctx, w_mod, b_mod, norm_ffn1, ffn1_w_gate, ffn1_w_up, ffn1_w_down,
              norm_mix, w_in, b_gate, attn_sink, na_rpb, qk_norm_q, qk_norm_k,
              ssm_conv_w, ssm_conv_b, ssm_dt_bias, ssm_a_log, ssm_d, ssm_norm,
              w_branch, w_out, norm_ffn2, ffn2_w_gate, ffn2_w_up, ffn2_w_down, final_norm):
    cos, sin = rope_tables(x.shape[1])
    c_act = jax.nn.silu(c)
    cc_act = jax.nn.silu(c_ctx)
    xc = ctx
    for l in range(DEPTH):
        lp = {
            'w_mod': w_mod[l], 'b_mod': b_mod[l],
            'norm_ffn1': norm_ffn1[l], 'ffn1_w_gate': ffn1_w_gate[l], 'ffn1_w_up': ffn1_w_up[l],
            'ffn1_w_down': ffn1_w_down[l],
            'norm_mix': norm_mix[l], 'w_in': w_in[l], 'b_gate': b_gate[l],
            'attn_sink': attn_sink[l], 'na_rpb': na_rpb[l],
            'qk_norm_q': qk_norm_q[l], 'qk_norm_k': qk_norm_k[l],
            'ssm_conv_w': ssm_conv_w[l], 'ssm_conv_b': ssm_conv_b[l], 'ssm_dt_bias': ssm_dt_bias[l],
            'ssm_a_log': ssm_a_log[l], 'ssm_d': ssm_d[l], 'ssm_norm': ssm_norm[l],
            'w_branch': w_branch[l], 'w_out': w_out[l],
            'norm_ffn2': norm_ffn2[l], 'ffn2_w_gate': ffn2_w_gate[l], 'ffn2_w_up': ffn2_w_up[l],
            'ffn2_w_down': ffn2_w_down[l],
        }
        x, xc = layer(x, xc, c_act, cc_act, lp, cos, sin, l < DEPTH - 1)
    return rms_norm(x, final_norm)
```

```python
import functools
import math

import numpy as np
import jax
import jax.numpy as jnp
from jax import lax
from jax.experimental import pallas as pl
from jax.experimental.pallas import tpu as pltpu

D_MODEL = 1024
DEPTH = 4
CTX_LEN = 256
GRID_W = 64
HEAD_DIM = 64
ROPE_THETA = 10000.0
EPS = 1e-6
NEG_INF = -1e30
D_FF = 2816
N_BRANCH = 4
BRANCH_W = 256
WA_WINDOW = 128
NA_ROWS = 8
NA_COLS = 16
SSM_HEADS = 4
SSM_GROUPS = 2
SSM_STATE = 128
SSM_CONV = 5
SSM_CHUNK = 128
SSM_INNER = 256
SSM_CONV_CH = 768
N_MOD = 9
MOD_ROWS = 16
LANE = 128
DT_PAD = LANE
QK_SCALE = HEAD_DIM ** -0.5
VMEM_LIMIT = 56 * 1024 * 1024

BF16 = jnp.bfloat16
F32 = jnp.float32


def _cparams(sem):
    return pltpu.CompilerParams(dimension_semantics=sem, vmem_limit_bytes=VMEM_LIMIT)


def _const_spec(shape):
    nd = len(shape)
    return pl.BlockSpec(shape, lambda *_: (0,) * nd)


def _dot(a, b):
    return jnp.dot(a, b, preferred_element_type=F32)


def _dot_nt(a, b):
    return lax.dot_general(a, b, (((1,), (1,)), ((), ())), preferred_element_type=F32)


def _silu(t):
    return t * (1.0 / (1.0 + jnp.exp(-t)))


def _rms(x, g):
    return (x * lax.rsqrt(jnp.mean(x * x, axis=-1, keepdims=True) + EPS)) * g


def _ada(x, g, shift, scale):
    return _rms(x, g) * (1.0 + scale) + shift


def _mod_kernel(c_ref, w_ref, b_ref, o_ref):
    act = _silu(c_ref[...]).astype(BF16)
    o_ref[...] = _dot(act, w_ref[...].astype(BF16)) + b_ref[...]


def _modulation(c_rows, w_mod, b_mod):
    return pl.pallas_call(
        _mod_kernel,
        out_shape=jax.ShapeDtypeStruct((DEPTH, N_MOD, MOD_ROWS, D_MODEL), F32),
        grid=(DEPTH, N_MOD),
        in_specs=[
            pl.BlockSpec((MOD_ROWS, D_MODEL), lambda l, i: (0, 0)),
            pl.BlockSpec((None, D_MODEL, D_MODEL), lambda l, i: (l, 0, i)),
            pl.BlockSpec((None, 1, D_MODEL), lambda l, i: (l, 0, i)),
        ],
        out_specs=pl.BlockSpec((None, None, MOD_ROWS, D_MODEL), lambda l, i: (l, i, 0, 0)),
        compiler_params=_cparams(("arbitrary", "arbitrary")),
        name="modulation",
    )(c_rows, w_mod, b_mod.reshape(DEPTH, 1, N_MOD * D_MODEL))


def _mod_spec(kind):
    return pl.BlockSpec((None, None, 1, D_MODEL), lambda b, t: (kind, b, 0, 0))


FFN_TM = 512
FFN_CHUNKS = ((0, 1024), (1024, 2048), (2048, 2816))


def _ffn_kernel(x_ref, sh_ref, sc_ref, gt_ref, g_ref, wg_ref, wu_ref, wd_ref, *rest, final):
    if final:
        fg_ref, o_ref = rest
    else:
        (o_ref,) = rest
    x = x_ref[...]
    xn = _ada(x, g_ref[...], sh_ref[...], sc_ref[...]).astype(BF16)
    acc = None
    for c0, c1 in FFN_CHUNKS:
        gate = _dot(xn, wg_ref[:, c0:c1])
        up = _dot(xn, wu_ref[:, c0:c1])
        h = (_silu(gate) * up).astype(BF16)
        part = _dot(h, wd_ref[c0:c1, :])
        acc = part if acc is None else acc + part
    y = x + (0.5 * gt_ref[...]) * acc
    if final:
        y = _rms(y, fg_ref[...])
    o_ref[...] = y


def _ffn(x_all, mod_l, kinds, gain, wg, wu, wd, nb, final_gain=None):
    seq = x_all.shape[1]
    final = final_gain is not None
    in_specs = [
        pl.BlockSpec((None, FFN_TM, D_MODEL), lambda b, t: (b, t, 0)),
        _mod_spec(kinds[0]), _mod_spec(kinds[1]), _mod_spec(kinds[2]),
        _const_spec((1, D_MODEL)),
        _const_spec((D_MODEL, D_FF)), _const_spec((D_MODEL, D_FF)), _const_spec((D_FF, D_MODEL)),
    ]
    args = [x_all, mod_l, mod_l, mod_l, gain.reshape(1, D_MODEL), wg, wu, wd]
    if final:
        in_specs.append(_const_spec((1, D_MODEL)))
        args.append(final_gain.reshape(1, D_MODEL))
    return pl.pallas_call(
        functools.partial(_ffn_kernel, final=final),
        out_shape=jax.ShapeDtypeStruct((nb, seq, D_MODEL), F32),
        grid=(nb, seq // FFN_TM),
        in_specs=in_specs,
        out_specs=pl.BlockSpec((None, FFN_TM, D_MODEL), lambda b, t: (b, t, 0)),
        compiler_params=_cparams(("arbitrary", "arbitrary")),
        name="ffn",
    )(*args)


PA, PB, PC, PZ, PX, PDT = 0, 512, 1280, 1792, 2048, 2816
PROJ_N = PDT + DT_PAD
PROJ_TM = 512
Q_HEAD_ORDER = (0, 2, 1, 3)


def _lane_iota(shape):
    return lax.broadcasted_iota(jnp.int32, shape, len(shape) - 1)


def _rope(t, cos, sin):
    w = t.shape[-1]
    first = (_lane_iota((1, w)) % 32) < 16
    rot = jnp.where(first, pltpu.roll(t, w - 16, 1), pltpu.roll(t, 16, 1))
    return t * cos[:, :w] + rot * sin[:, :w]


def _head_norm(t, gain):
    w = t.shape[-1]
    lane = _lane_iota((1, w))
    sq = t * t
    scale = jnp.zeros_like(t)
    for h in range(w // HEAD_DIM):
        m = (lane >= h * HEAD_DIM) & (lane < (h + 1) * HEAD_DIM)
        ms = jnp.sum(jnp.where(m, sq, 0.0), axis=-1, keepdims=True) * (1.0 / HEAD_DIM)
        scale = jnp.where(m, lax.rsqrt(ms + EPS), scale)
    return (t * scale) * gain[:, :w]


def _proj_kernel(x_ref, sh_ref, sc_ref, g_ref, w_ref, rope_ref, nq_ref, nk_ref,
                 qa_ref, qb_ref, qc_ref, z_ref, xbc_ref, dt_ref):
    xn = _ada(x_ref[...], g_ref[...], sh_ref[...], sc_ref[...]).astype(BF16)
    cos = rope_ref[:, 0:256]
    sin = rope_ref[:, 256:512]
    a = _dot(xn, w_ref[:, PA:PA + 512])
    qa_ref[:, 0:256] = (_rope(a[:, 0:256], cos, sin) * QK_SCALE).astype(BF16)
    qa_ref[:, 256:384] = _rope(a[:, 256:384], cos, sin).astype(BF16)
    qa_ref[:, 384:512] = a[:, 384:512].astype(BF16)
    b = _dot(xn, w_ref[:, PB:PB + 768])
    qb_ref[:, 0:256] = (b[:, 0:256] * QK_SCALE).astype(BF16)
    qb_ref[:, 256:768] = b[:, 256:768].astype(BF16)
    c = _dot(xn, w_ref[:, PC:PC + 512])
    qc_ref[:, 0:256] = (_rope(_head_norm(c[:, 0:256], nq_ref[...]), cos, sin) * QK_SCALE).astype(BF16)
    qc_ref[:, 256:384] = _rope(_head_norm(c[:, 256:384], nk_ref[...]), cos, sin).astype(BF16)
    qc_ref[:, 384:512] = c[:, 384:512].astype(BF16)
    z_ref[...] = _dot(xn, w_ref[:, PZ:PZ + 256])
    xbc_ref[...] = _dot(xn, w_ref[:, PX:PX + 768])
    dt_ref[...] = _dot(xn, w_ref[:, PDT:PDT + DT_PAD])


def _proj(x_all, mod_l, gain, w_proj, rope_tab, nq, nk, nlat):
    nb, seq, _ = x_all.shape
    tok = lambda w: pl.BlockSpec((None, PROJ_TM, w), lambda b, t: (b, t, 0))
    outs = [(512, BF16), (768, BF16), (512, BF16), (256, F32), (768, F32), (DT_PAD, F32)]
    return pl.pallas_call(
        _proj_kernel,
        out_shape=[jax.ShapeDtypeStruct((nb, seq, w), dt) for w, dt in outs],
        grid=(nb, seq // PROJ_TM),
        in_specs=[
            tok(D_MODEL), _mod_spec(3), _mod_spec(4),
            _const_spec((1, D_MODEL)),
            _const_spec((D_MODEL, PROJ_N)),
            pl.BlockSpec((None, PROJ_TM, 512), lambda b, t: (b // nlat, t, 0)),
            _const_spec((1, 256)), _const_spec((1, 256)),
        ],
        out_specs=[tok(w) for w, _ in outs],
        compiler_params=_cparams(("arbitrary", "arbitrary")),
        name="in_proj",
    )(x_all, mod_l, mod_l, gain.reshape(1, D_MODEL), w_proj, rope_tab, nq, nk)


def _stack_heads(q):
    lo = _lane_iota((1, LANE)) < HEAD_DIM
    t0, t1 = q[:, 0:LANE], q[:, LANE:2 * LANE]
    zero = jnp.zeros_like(t0)
    return jnp.concatenate([jnp.where(lo, t0, zero), jnp.where(lo, t1, zero),
                            jnp.where(lo, zero, t0), jnp.where(lo, zero, t1)], axis=0)


def _unstack_heads(o, t):
    lo = _lane_iota((1, LANE)) < HEAD_DIM
    return jnp.concatenate([jnp.where(lo, o[0:t], o[2 * t:3 * t]),
                            jnp.where(lo, o[t:2 * t], o[3 * t:4 * t])], axis=1)


def _softmax_pv(scores, values, extra_logit=None):
    m = None
    for s in scores:
        sm = jnp.max(s, axis=-1, keepdims=True)
        m = sm if m is None else jnp.maximum(m, sm)
    if extra_logit is not None:
        m = jnp.maximum(m, extra_logit)
    den = None
    acc = None
    for s, v in zip(scores, values):
        p = jnp.exp(s - m)
        ps = jnp.sum(p, axis=-1, keepdims=True)
        den = ps if den is None else den + ps
        pv = _dot(p.astype(BF16), v)
        acc = pv if acc is None else acc + pv
    if extra_logit is not None:
        den = den + jnp.exp(extra_logit - m)
    return acc * (1.0 / den)


def _sink_column(sink_ref, t):
    row = lax.broadcasted_iota(jnp.int32, (4 * t, 1), 0)
    col = jnp.full((4 * t, 1), sink_ref[3], F32)
    for j in (2, 1, 0):
        col = jnp.where(row < (j + 1) * t, sink_ref[j], col)
    return col


ATT_TQ = 128


def _gqa_kernel(*refs, window, seq):
    if window:
        sink_ref, q_ref, k_ref, v_ref, kc_ref, vc_ref, o_ref = refs
    else:
        q_ref, k_ref, v_ref, kc_ref, vc_ref, o_ref = refs
    tq = ATT_TQ
    qs = _stack_heads(q_ref[...])
    if window:
        nkw = tq + 2 * WA_WINDOW
        i = pl.program_id(1)
        start = pl.multiple_of(jnp.clip(i * tq - WA_WINDOW, 0, seq - nkw), LANE)
        kw = k_ref[pl.ds(start, nkw), :]
        vw = v_ref[pl.ds(start, nkw), :]
        qpos = i * tq + lax.broadcasted_iota(jnp.int32, (4 * tq, nkw), 0) % tq
        kpos = start + lax.broadcasted_iota(jnp.int32, (4 * tq, nkw), 1)
        s_lat = jnp.where(jnp.abs(qpos - kpos) <= WA_WINDOW, _dot_nt(qs, kw), NEG_INF)
        extra = _sink_column(sink_ref, tq)
    else:
        vw = v_ref[...]
        s_lat = _dot_nt(qs, k_ref[...])
        extra = None
    s_ctx = _dot_nt(qs, kc_ref[...])
    o = _softmax_pv([s_lat, s_ctx], [vw, vc_ref[...]], extra)
    o_ref[...] = _unstack_heads(o, tq).astype(BF16)


def _gqa_attention(qkv, sink, nlat, window):
    nb, seq, _ = qkv.shape
    in_specs = [
        pl.BlockSpec((None, ATT_TQ, 256), lambda b, i: (b, i, 0)),
        pl.BlockSpec((None, seq, LANE), lambda b, i: (b, 0, 2)),
        pl.BlockSpec((None, seq, LANE), lambda b, i: (b, 0, 3)),
        pl.BlockSpec((None, CTX_LEN, LANE), lambda b, i: (nlat, b, 2)),
        pl.BlockSpec((None, CTX_LEN, LANE), lambda b, i: (nlat, b, 3)),
    ]
    args = [qkv, qkv, qkv, qkv, qkv]
    if window:
        in_specs.insert(0, pl.BlockSpec(memory_space=pltpu.SMEM))
        args.insert(0, sink)
    return pl.pallas_call(
        functools.partial(_gqa_kernel, window=window, seq=seq),
        out_shape=jax.ShapeDtypeStruct((nb, seq, BRANCH_W), BF16),
        grid=(nlat, seq // ATT_TQ),
        in_specs=in_specs,
        out_specs=pl.BlockSpec((None, ATT_TQ, BRANCH_W), lambda b, i: (b, i, 0)),
        compiler_params=_cparams(("arbitrary", "arbitrary")),
        name="window_attention" if window else "dense_attention",
    )(*args)


NA_KROWS = NA_ROWS * GRID_W


def _na_kernel(q_ref, k_ref, v_ref, kc_ref, vc_ref, bias_ref, o_ref, *, rows):
    r = pl.program_id(1)
    rs = jnp.clip(r - NA_ROWS // 2, 0, rows - NA_ROWS)
    pat = rs - r + (NA_ROWS - 1)
    start = pl.multiple_of(rs * GRID_W, GRID_W)
    lo = _lane_iota((1, LANE)) < HEAD_DIM
    outs = []
    for blk in range(2):
        cols = slice(blk * LANE, (blk + 1) * LANE)
        qb = q_ref[:, cols]
        zero = jnp.zeros_like(qb)
        qs = jnp.concatenate([jnp.where(lo, qb, zero), jnp.where(lo, zero, qb)], axis=0)
        kw = k_ref[pl.ds(start, NA_KROWS), cols]
        vw = v_ref[pl.ds(start, NA_KROWS), cols]
        s_nb = _dot_nt(qs, kw) + bias_ref[pat, blk]
        s_ctx = _dot_nt(qs, kc_ref[:, cols])
        o = _softmax_pv([s_nb, s_ctx], [vw, vc_ref[:, cols]])
        outs.append(jnp.where(lo, o[0:GRID_W], o[GRID_W:2 * GRID_W]))
    o_ref[...] = jnp.concatenate(outs, axis=1).astype(BF16)


def _na_bias_table(rpb):
    c = jnp.arange(GRID_W)
    qstart = jnp.clip(c - NA_COLS // 2, 0, GRID_W - NA_COLS)
    ok = (c[None, :] >= qstart[:, None]) & (c[None, :] < qstart[:, None] + NA_COLS)
    dc = jnp.clip(c[None, :] - c[:, None] + NA_COLS - 1, 0, 2 * NA_COLS - 2)
    band = jnp.where(ok[None, None], rpb[:, :, dc], NEG_INF)
    dr = jnp.arange(NA_ROWS)[:, None] + jnp.arange(NA_ROWS)[None, :]
    t = band[:, dr]
    t = t.transpose(1, 0, 3, 2, 4).reshape(NA_ROWS, 2, 2 * GRID_W, NA_KROWS)
    return t.astype(F32)


def _na_attention(qkv, bias, nlat):
    nb, seq, _ = qkv.shape
    rows = seq // GRID_W
    return pl.pallas_call(
        functools.partial(_na_kernel, rows=rows),
        out_shape=jax.ShapeDtypeStruct((nb, seq, BRANCH_W), BF16),
        grid=(nlat, rows),
        in_specs=[
            pl.BlockSpec((None, GRID_W, 256), lambda b, r: (b, r, 0)),
            pl.BlockSpec((None, seq, 256), lambda b, r: (b, 0, 1)),
            pl.BlockSpec((None, seq, 256), lambda b, r: (b, 0, 2)),
            pl.BlockSpec((None, CTX_LEN, 256), lambda b, r: (nlat, b, 1)),
            pl.BlockSpec((None, CTX_LEN, 256), lambda b, r: (nlat, b, 2)),
            _const_spec(bias.shape),
        ],
        out_specs=pl.BlockSpec((None, GRID_W, BRANCH_W), lambda b, r: (b, r, 0)),
        compiler_params=_cparams(("arbitrary", "arbitrary")),
        name="neighborhood_attention",
    )(qkv, qkv, qkv, qkv, qkv, bias)


def _ctx_kernel(sink_ref, qa_ref, ka_ref, va_ref, qb_ref, kb_ref, vb_ref, qc_ref, kc_ref, vc_ref, yd_ref,
                _a, _b, _c, _d, oa_ref, ob_ref, oc_ref, od_ref):
    t = CTX_LEN
    qs = _stack_heads(qa_ref[...])
    o = _softmax_pv([_dot_nt(qs, ka_ref[...])], [va_ref[...]], _sink_column(sink_ref, t))
    oa_ref[...] = _unstack_heads(o, t).astype(BF16)
    qs = _stack_heads(qc_ref[...])
    o = _softmax_pv([_dot_nt(qs, kc_ref[...])], [vc_ref[...]])
    oc_ref[...] = _unstack_heads(o, t).astype(BF16)
    lo = _lane_iota((1, LANE)) < HEAD_DIM
    outs = []
    for blk in range(2):
        cols = slice(blk * LANE, (blk + 1) * LANE)
        qb = qb_ref[:, cols]
        zero = jnp.zeros_like(qb)
        qs = jnp.concatenate([jnp.where(lo, qb, zero), jnp.where(lo, zero, qb)], axis=0)
        o = _softmax_pv([_dot_nt(qs, kb_ref[:, cols])], [vb_ref[:, cols]])
        outs.append(jnp.where(lo, o[0:t], o[t:2 * t]))
    ob_ref[...] = jnp.concatenate(outs, axis=1).astype(BF16)
    od_ref[...] = yd_ref[...]


def _ctx_attention(sink, qa, qb, qc, yd_ctx, ya, yb, yc, yd, nlat):
    blk = lambda w, j: pl.BlockSpec((None, CTX_LEN, w), lambda b: (nlat, b, j))
    anyspec = pl.BlockSpec(memory_space=pl.ANY)
    out_spec = pl.BlockSpec((None, CTX_LEN, BRANCH_W), lambda b: (nlat, b, 0))
    return pl.pallas_call(
        _ctx_kernel,
        out_shape=[jax.ShapeDtypeStruct(ya.shape, BF16)] * 4,
        grid=(nlat,),
        in_specs=[
            pl.BlockSpec(memory_space=pltpu.SMEM),
            blk(256, 0), blk(LANE, 2), blk(LANE, 3),
            blk(256, 0), blk(256, 1), blk(256, 2),
            blk(256, 0), blk(LANE, 2), blk(LANE, 3),
            pl.BlockSpec((None, CTX_LEN, BRANCH_W), lambda b: (b, 0, 0)),
            anyspec, anyspec, anyspec, anyspec,
        ],
        out_specs=[out_spec] * 4,
        input_output_aliases={11: 0, 12: 1, 13: 2, 14: 3},
        compiler_params=_cparams(("arbitrary",)),
        name="context_attention",
    )(sink, qa, qa, qa, qb, qb, qb, qc, qc, qc, yd_ctx, ya, yb, yc, yd)


CONV_PAD = 8
FIN_ROWS = 256


def _ssd_kernel(xbc_ref, xbcc_ref, dt_ref, dtc_ref, z_ref, zc_ref, cw_ref, cb_ref, dtb_ref, alog_ref,
                dvec_ref, nrm_ref, y_ref, yc_ref, pad_ref, u_ref, dts_ref, yacc_ref, st_ref, *, seq):
    ltot = CTX_LEN + seq
    nchunk = ltot // SSM_CHUNK
    nctx = CTX_LEN // SSM_CHUNK
    half = SSM_CONV // 2

    def conv_into(src_ref, n, dst):
        pad_ref[0:CONV_PAD, :] = jnp.zeros((CONV_PAD, SSM_CONV_CH), F32)
        pad_ref[CONV_PAD + n:2 * CONV_PAD + n, :] = jnp.zeros((CONV_PAD, SSM_CONV_CH), F32)
        pad_ref[CONV_PAD:CONV_PAD + n, :] = src_ref[...]
        for r0 in range(0, n, FIN_ROWS):
            acc = jnp.broadcast_to(cb_ref[...], (FIN_ROWS, SSM_CONV_CH))
            for k in range(SSM_CONV):
                acc = acc + pad_ref[r0 + CONV_PAD - half + k:r0 + CONV_PAD - half + k + FIN_ROWS, :] * cw_ref[k:k + 1, :]
            u_ref[dst + r0:dst + r0 + FIN_ROWS, :] = _silu(acc)

    conv_into(xbcc_ref, CTX_LEN, 0)
    conv_into(xbc_ref, seq, CTX_LEN)

    def softplus(t):
        return jnp.maximum(t, 0.0) + jnp.log1p(jnp.exp(-jnp.abs(t)))

    dts_ref[0:CTX_LEN, :] = softplus(dtc_ref[...] + dtb_ref[...])
    dts_ref[CTX_LEN:ltot, :] = softplus(dt_ref[...] + dtb_ref[...])
    a_row = -jnp.exp(alog_ref[...])

    row = lax.broadcasted_iota(jnp.int32, (SSM_CHUNK, SSM_CHUNK), 0)
    col = lax.broadcasted_iota(jnp.int32, (SSM_CHUNK, SSM_CHUNK), 1)
    lo = _lane_iota((1, LANE)) < SSM_STATE // 2

    def pick(t, j0):
        return jnp.where(lo, t[:, j0:j0 + 1], t[:, j0 + 1:j0 + 2])

    def chunk_step(cidx, direction):
        fwd = direction == 0
        causal = (col <= row) if fwd else (col >= row)
        tri = jnp.where(causal, 1.0, 0.0).astype(BF16)
        r0 = pl.multiple_of(cidx * SSM_CHUNK, SSM_CHUNK)
        dt = dts_ref[pl.ds(r0, SSM_CHUNK), :]
        da = dt * a_row
        hi = da.astype(BF16)
        r1 = da - hi.astype(F32)
        mid = r1.astype(BF16)
        low = (r1 - mid.astype(F32)).astype(BF16)
        cum = _dot(tri, hi) + _dot(tri, mid) + _dot(tri, low)
        cum_t = cum.T
        tot = cum[SSM_CHUNK - 1:SSM_CHUNK, :] if fwd else cum[0:1, :]
        dec = jnp.exp(tot - cum)
        ecum = jnp.exp(cum)
        etot = jnp.exp(tot)
        for g in range(SSM_GROUPS):
            gl = slice(g * LANE, (g + 1) * LANE)
            xg = u_ref[pl.ds(r0, SSM_CHUNK), gl]
            bg = u_ref[pl.ds(r0, SSM_CHUNK), SSM_INNER + g * SSM_STATE:SSM_INNER + (g + 1) * SSM_STATE]
            cg = u_ref[pl.ds(r0, SSM_CHUNK), SSM_INNER + (SSM_GROUPS + g) * SSM_STATE:
                       SSM_INNER + (SSM_GROUPS + g + 1) * SSM_STATE].astype(BF16)
            bt = bg.T.astype(BF16)
            cb = _dot(cg, bt)
            j0 = direction * SSM_HEADS + 2 * g
            xdt = xg * pick(dt, j0)
            y = _dot(cg, st_ref[:, gl].astype(BF16)) * pick(ecum, j0)
            for hh in range(2):
                j = j0 + hh
                seg = jnp.where(causal, cum[:, j:j + 1] - cum_t[j:j + 1, :], NEG_INF)
                mh = (cb * jnp.exp(seg)).astype(BF16)
                xh = jnp.where(lo, xdt, 0.0) if hh == 0 else jnp.where(lo, 0.0, xdt)
                y = y + _dot(mh, xh.astype(BF16))
            st_ref[:, gl] = st_ref[:, gl] * pick(etot, j0) + _dot(bt, (xdt * pick(dec, j0)).astype(BF16))
            if fwd:
                yacc_ref[pl.ds(r0, SSM_CHUNK), gl] = y
            else:
                yacc_ref[pl.ds(r0, SSM_CHUNK), gl] = yacc_ref[pl.ds(r0, SSM_CHUNK), gl] + y

    st_ref[...] = jnp.zeros_like(st_ref)

    def fwd_body(t, carry):
        chunk_step(t, 0)
        return carry

    lax.fori_loop(0, nchunk, fwd_body, 0)

    st_ref[...] = jnp.zeros_like(st_ref)

    def bwd_body(t, carry):
        chunk_step(jnp.where(t < nctx, nctx - 1 - t, nchunk + nctx - 1 - t), 1)
        return carry

    lax.fori_loop(0, nchunk, bwd_body, 0)

    def finish(z_src, out_ref, n, base):
        for r0 in range(0, n, FIN_ROWS):
            y = yacc_ref[base + r0:base + r0 + FIN_ROWS, :] + dvec_ref[...] * u_ref[base + r0:base + r0 + FIN_ROWS, 0:SSM_INNER]
            y = y * _silu(z_src[r0:r0 + FIN_ROWS, :])
            out_ref[r0:r0 + FIN_ROWS, :] = _rms(y, nrm_ref[...]).astype(BF16)

    finish(zc_ref, yc_ref, CTX_LEN, 0)
    finish(z_ref, y_ref, seq, CTX_LEN)


def _ssd(xbc, dt, z, conv_w, conv_b, dt_bias, a_log, dvec, nrm, nlat):
    nb, seq, _ = xbc.shape
    ltot = CTX_LEN + seq
    lat = lambda w: pl.BlockSpec((None, seq, w), lambda b: (b, 0, 0))
    ctx = lambda w: pl.BlockSpec((None, CTX_LEN, w), lambda b: (nlat, b, 0))
    return pl.pallas_call(
        functools.partial(_ssd_kernel, seq=seq),
        out_shape=[jax.ShapeDtypeStruct((nb, seq, SSM_INNER), BF16),
                   jax.ShapeDtypeStruct((nlat, CTX_LEN, SSM_INNER), BF16)],
        grid=(nlat,),
        in_specs=[
            lat(SSM_CONV_CH), ctx(SSM_CONV_CH), lat(DT_PAD), ctx(DT_PAD), lat(SSM_INNER), ctx(SSM_INNER),
            _const_spec((8, SSM_CONV_CH)), _const_spec((1, SSM_CONV_CH)),
            _const_spec((1, DT_PAD)), _const_spec((1, DT_PAD)),
            _const_spec((1, SSM_INNER)), _const_spec((1, SSM_INNER)),
        ],
        out_specs=[pl.BlockSpec((None, seq, SSM_INNER), lambda b: (b, 0, 0)),
                   pl.BlockSpec((None, CTX_LEN, SSM_INNER), lambda b: (b, 0, 0))],
        scratch_shapes=[
            pltpu.VMEM((seq + 2 * CONV_PAD, SSM_CONV_CH), F32),
            pltpu.VMEM((ltot, SSM_CONV_CH), F32),
            pltpu.VMEM((ltot, DT_PAD), F32),
            pltpu.VMEM((ltot, SSM_INNER), F32),
            pltpu.VMEM((SSM_STATE, SSM_INNER), F32),
        ],
        compiler_params=_cparams(("arbitrary",)),
        name="ssd",
    )(xbc, xbc, dt, dt, z, z, conv_w, conv_b, dt_bias, a_log, dvec, nrm)


MERGE_TM = 512


def _merge_kernel(x_ref, sh_ref, sc_ref, gt_ref, g_ref, ya_ref, yb_ref, yc_ref, yd_ref,
                  wgl_ref, bg_ref, wbr_ref, wo_ref, o_ref):
    x = x_ref[...]
    xn = _ada(x, g_ref[...], sh_ref[...], sc_ref[...]).astype(BF16)
    mix = None
    for k, y_ref in enumerate((ya_ref, yb_ref, yc_ref, yd_ref)):
        cols = slice(k * D_MODEL, (k + 1) * D_MODEL)
        pre = _dot(xn, wgl_ref[:, cols]) + bg_ref[:, cols]
        gate = 1.0 / (1.0 + jnp.exp(-pre))
        term = gate * _dot(y_ref[...], wbr_ref[k])
        mix = term if mix is None else mix + term
    o_ref[...] = x + gt_ref[...] * _dot(mix.astype(BF16), wo_ref[...])


def _merge(x_all, mod_l, gain, ys, w_gate, b_gate, w_branch, w_out, nb):
    seq = x_all.shape[1]
    tok = lambda w: pl.BlockSpec((None, MERGE_TM, w), lambda b, t: (b, t, 0))
    return pl.pallas_call(
        _merge_kernel,
        out_shape=jax.ShapeDtypeStruct((nb, seq, D_MODEL), F32),
        grid=(nb, seq // MERGE_TM),
        in_specs=[
            tok(D_MODEL), _mod_spec(3), _mod_spec(4), _mod_spec(5),
            _const_spec((1, D_MODEL)),
            tok(BRANCH_W), tok(BRANCH_W), tok(BRANCH_W), tok(BRANCH_W),
            _const_spec((D_MODEL, N_BRANCH * D_MODEL)), _const_spec((1, N_BRANCH * D_MODEL)),
            _const_spec((N_BRANCH, BRANCH_W, D_MODEL)), _const_spec((D_MODEL, D_MODEL)),
        ],
        out_specs=tok(D_MODEL),
        compiler_params=_cparams(("arbitrary", "arbitrary")),
        name="merge",
    )(x_all, mod_l, mod_l, mod_l, gain.reshape(1, D_MODEL), *ys,
      w_gate, b_gate.reshape(1, N_BRANCH * D_MODEL), w_branch, w_out)


def _rope_table(seq):
    pos = jnp.arange(seq)
    axes = jnp.stack([pos // GRID_W, pos % GRID_W], axis=-1).astype(F32)
    quarter = HEAD_DIM // 4
    inv = 1.0 / (ROPE_THETA ** (jnp.arange(quarter, dtype=F32) * 4.0 / HEAD_DIM))
    ang = axes[:, :, None] * inv
    cos, sin = jnp.cos(ang), jnp.sin(ang)
    cos_h = jnp.concatenate([cos, cos], axis=-1).reshape(seq, HEAD_DIM)
    sin_h = jnp.concatenate([-sin, sin], axis=-1).reshape(seq, HEAD_DIM)
    lat = jnp.concatenate([jnp.tile(cos_h, (1, 4)), jnp.tile(sin_h, (1, 4))], axis=-1)
    ident = jnp.concatenate([jnp.ones((seq, 256), F32), jnp.zeros((seq, 256), F32)], axis=-1)
    return jnp.stack([lat, ident])


def _permute_heads(w, axis):
    parts = jnp.split(w, 4, axis=axis)
    return jnp.concatenate([parts[j] for j in Q_HEAD_ORDER], axis=axis)


def _pack_w_in(w_in):
    aq, ak, av, bq, bk, bv, cq, ck, cv, z, xbc, dt, gl = jnp.split(
        w_in, np.cumsum((256, 128, 128, 256, 256, 256, 256, 128, 128, 256, 768, 8, 4096))[:-1].tolist(), axis=1)
    dt = jnp.pad(dt, ((0, 0), (0, DT_PAD - dt.shape[1])))
    proj = jnp.concatenate([_permute_heads(aq, 1), ak, av, bq, bk, bv, _permute_heads(cq, 1), ck, cv, z, xbc, dt],
                           axis=1)
    return proj.astype(BF16), gl.astype(BF16)


def kernel(x, c, ctx, c_ctx, w_mod, b_mod, norm_ffn1, ffn1_w_gate, ffn1_w_up, ffn1_w_down, norm_mix, w_in, b_gate,
           attn_sink, na_rpb, qk_norm_q, qk_norm_k, ssm_conv_w, ssm_conv_b, ssm_dt_bias, ssm_a_log, ssm_d, ssm_norm,
           w_branch, w_out, norm_ffn2, ffn2_w_gate, ffn2_w_up, ffn2_w_down, final_norm):
    nlat, seq, _ = x.shape
    assert nlat * CTX_LEN == seq and nlat + 1 <= MOD_ROWS
    nb = nlat + 1
    x_all = jnp.concatenate([x, ctx.reshape(1, seq, D_MODEL)], axis=0)
    c_rows = jnp.concatenate([c, c_ctx[None], jnp.zeros((MOD_ROWS - nb, D_MODEL), F32)], axis=0)
    mod = _modulation(c_rows, w_mod, b_mod).reshape(DEPTH, N_MOD, MOD_ROWS, 1, D_MODEL)
    rope_tab = _rope_table(seq)
    pad_lane = lambda v: jnp.pad(v.reshape(1, -1), ((0, 0), (0, DT_PAD - v.size)))

    for l in range(DEPTH):
        with_ctx = l < DEPTH - 1
        last = l == DEPTH - 1
        w_proj, w_gl = _pack_w_in(w_in[l])
        wbr = w_branch[l]
        wbr = jnp.stack([_permute_heads(wbr[0], 0), wbr[1], _permute_heads(wbr[2], 0), wbr[3]]).astype(BF16)
        nq = jnp.tile(qk_norm_q[l], 4).reshape(1, 256)
        nk = jnp.tile(qk_norm_k[l], 4).reshape(1, 256)
        conv_w = jnp.pad(ssm_conv_w[l], ((0, 8 - SSM_CONV), (0, 0)))
        dvec = jnp.repeat(ssm_d[l], SSM_INNER // SSM_HEADS).reshape(1, SSM_INNER)

        x_all = _ffn(x_all, mod[l], (0, 1, 2), norm_ffn1[l], ffn1_w_gate[l].astype(BF16),
                     ffn1_w_up[l].astype(BF16), ffn1_w_down[l].astype(BF16), nb)
        qa, qb, qc, z, xbc, dt = _proj(x_all, mod[l], norm_mix[l], w_proj, rope_tab, nq, nk, nlat)
        ya = _gqa_attention(qa, attn_sink[l], nlat, window=True)
        yb = _na_attention(qb, _na_bias_table(na_rpb[l]), nlat)
        yc = _gqa_attention(qc, None, nlat, window=False)
        yd, yd_ctx = _ssd(xbc, dt, z, conv_w, ssm_conv_b[l].reshape(1, -1), pad_lane(ssm_dt_bias[l]),
                          pad_lane(ssm_a_log[l]), dvec, ssm_norm[l].reshape(1, -1), nlat)
        nbm = nb if with_ctx else nlat
        if with_ctx:
            ya, yb, yc, yd = _ctx_attention(attn_sink[l], qa, qb, qc, yd_ctx, ya, yb, yc, yd, nlat)
        x_all = _merge(x_all, mod[l], norm_mix[l], (ya, yb, yc, yd), w_gl, b_gate[l], wbr,
                       w_out[l].astype(BF16), nbm)
        x_all = _ffn(x_all, mod[l], (6, 7, 8), norm_ffn2[l], ffn2_w_gate[l].astype(BF16),
                     ffn2_w_up[l].astype(BF16), ffn2_w_down[l].astype(BF16), nbm,
                     final_gain=final_norm if last else None)
    return x_all
```

```python
import functools

import jax
import jax.numpy as jnp
from jax import lax
from jax.experimental import pallas as pl
from jax.experimental.pallas import tpu as pltpu

D_MODEL = 1024
DEPTH = 4
CTX_LEN = 256
GRID_W = 64
HEAD_DIM = 64
ROPE_THETA = 10000.0
EPS = 1e-6
NEG_INF = -1e30
D_FF = 2816
N_BRANCH = 4
BRANCH_W = 256
WA_WINDOW = 128
NA_ROWS = 8
NA_COLS = 16
SSM_HEADS = 4
SSM_GROUPS = 2
SSM_STATE = 128
SSM_CONV = 5
SSM_CHUNK = 128
SSM_INNER = 256
SSM_CONV_CH = 768
N_MOD = 9
MOD_ROWS = 16
LANE = 128
DT_PAD = LANE
QK_SCALE = HEAD_DIM ** -0.5
VMEM_LIMIT = 56 * 1024 * 1024

BF16 = jnp.bfloat16
F32 = jnp.float32


def _cparams(sem):
    return pltpu.CompilerParams(dimension_semantics=sem, vmem_limit_bytes=VMEM_LIMIT)


def _const_spec(shape):
    nd = len(shape)
    return pl.BlockSpec(shape, lambda *_: (0,) * nd)


def _layer_spec(shape, layer):
    nd = len(shape)
    return pl.BlockSpec((None,) + tuple(shape), lambda *_: (layer,) + (0,) * nd)


def _dot(a, b):
    return jnp.dot(a, b, preferred_element_type=F32)


def _dot_nt(a, b):
    return lax.dot_general(a, b, (((1,), (1,)), ((), ())), preferred_element_type=F32)


def _silu(t):
    return t * (1.0 / (1.0 + jnp.exp(-t)))


def _rms(x, g):
    return (x * lax.rsqrt(jnp.mean(x * x, axis=-1, keepdims=True) + EPS)) * g


def _ada(x, g, shift, scale):
    return _rms(x, g) * (1.0 + scale) + shift


def _mod_kernel(c_ref, w_ref, b_ref, o_ref):
    act = _silu(c_ref[...]).astype(BF16)
    o_ref[...] = _dot(act, w_ref[...].astype(BF16)) + b_ref[...]


def _modulation(c_rows, w_mod, b_mod):
    return pl.pallas_call(
        _mod_kernel,
        out_shape=jax.ShapeDtypeStruct((DEPTH, N_MOD, MOD_ROWS, D_MODEL), F32),
        grid=(DEPTH, N_MOD),
        in_specs=[
            pl.BlockSpec((MOD_ROWS, D_MODEL), lambda l, i: (0, 0)),
            pl.BlockSpec((None, D_MODEL, D_MODEL), lambda l, i: (l, 0, i)),
            pl.BlockSpec((None, 1, D_MODEL), lambda l, i: (l, 0, i)),
        ],
        out_specs=pl.BlockSpec((None, None, MOD_ROWS, D_MODEL), lambda l, i: (l, i, 0, 0)),
        compiler_params=_cparams(("arbitrary", "arbitrary")),
        name="modulation",
    )(c_rows, w_mod, b_mod.reshape(DEPTH, 1, N_MOD * D_MODEL))


def _mod_spec(kind):
    return pl.BlockSpec((None, None, 1, D_MODEL), lambda b, t: (kind, b, 0, 0))


FFN_TM = 512
FFN_CHUNKS = ((0, 1024), (1024, 2048), (2048, 2816))


def _ffn_kernel(x_ref, sh_ref, sc_ref, gt_ref, g_ref, wg_ref, wu_ref, wd_ref, *rest, final):
    if final:
        fg_ref, o_ref = rest
    else:
        (o_ref,) = rest
    x = x_ref[...]
    xn = _ada(x, g_ref[...], sh_ref[...], sc_ref[...]).astype(BF16)
    acc = None
    for c0, c1 in FFN_CHUNKS:
        gate = _dot(xn, wg_ref[:, c0:c1])
        up = _dot(xn, wu_ref[:, c0:c1])
        h = (_silu(gate) * up).astype(BF16)
        part = _dot(h, wd_ref[c0:c1, :])
        acc = part if acc is None else acc + part
    y = x + (0.5 * gt_ref[...]) * acc
    if final:
        y = _rms(y, fg_ref[...])
    o_ref[...] = y


def _ffn(x_all, mod_l, kinds, gain, wg, wu, wd, layer, nb, final_gain=None):
    seq = x_all.shape[1]
    final = final_gain is not None
    in_specs = [
        pl.BlockSpec((None, FFN_TM, D_MODEL), lambda b, t: (b, t, 0)),
        _mod_spec(kinds[0]), _mod_spec(kinds[1]), _mod_spec(kinds[2]),
        _const_spec((1, D_MODEL)),
        _layer_spec((D_MODEL, D_FF), layer), _layer_spec((D_MODEL, D_FF), layer),
        _layer_spec((D_FF, D_MODEL), layer),
    ]
    args = [x_all, mod_l, mod_l, mod_l, gain.reshape(1, D_MODEL), wg, wu, wd]
    if final:
        in_specs.append(_const_spec((1, D_MODEL)))
        args.append(final_gain.reshape(1, D_MODEL))
    return pl.pallas_call(
        functools.partial(_ffn_kernel, final=final),
        out_shape=jax.ShapeDtypeStruct((nb, seq, D_MODEL), F32),
        grid=(nb, seq // FFN_TM),
        in_specs=in_specs,
        out_specs=pl.BlockSpec((None, FFN_TM, D_MODEL), lambda b, t: (b, t, 0)),
        compiler_params=_cparams(("arbitrary", "arbitrary")),
        name="ffn",
    )(*args)


PA, PB, PC, PZ, PX, PDT = 0, 512, 1280, 1792, 2048, 2816
PROJ_N = PDT + DT_PAD
PROJ_TM = 512
Q_HEAD_ORDER = (0, 2, 1, 3)


def _lane_iota(shape):
    return lax.broadcasted_iota(jnp.int32, shape, len(shape) - 1)


def _rope(t, cos, sin):
    w = t.shape[-1]
    first = (_lane_iota((1, w)) % 32) < 16
    rot = jnp.where(first, pltpu.roll(t, w - 16, 1), pltpu.roll(t, 16, 1))
    return t * cos[:, :w] + rot * sin[:, :w]


def _head_norm(t, gain):
    w = t.shape[-1]
    lane = _lane_iota((1, w))
    sq = t * t
    scale = jnp.zeros_like(t)
    for h in range(w // HEAD_DIM):
        m = (lane >= h * HEAD_DIM) & (lane < (h + 1) * HEAD_DIM)
        ms = jnp.sum(jnp.where(m, sq, 0.0), axis=-1, keepdims=True) * (1.0 / HEAD_DIM)
        scale = jnp.where(m, lax.rsqrt(ms + EPS), scale)
    return (t * scale) * gain[:, :w]


def _proj_kernel(x_ref, sh_ref, sc_ref, g_ref, w_ref, rope_ref, nq_ref, nk_ref,
                 qa_ref, qb_ref, qc_ref, z_ref, xbc_ref, dt_ref):
    xn = _ada(x_ref[...], g_ref[...], sh_ref[...], sc_ref[...]).astype(BF16)
    cos = rope_ref[:, 0:256]
    sin = rope_ref[:, 256:512]
    a = _dot(xn, w_ref[:, PA:PA + 512])
    qa_ref[:, 0:256] = (_rope(a[:, 0:256], cos, sin) * QK_SCALE).astype(BF16)
    qa_ref[:, 256:384] = _rope(a[:, 256:384], cos, sin).astype(BF16)
    qa_ref[:, 384:512] = a[:, 384:512].astype(BF16)
    b = _dot(xn, w_ref[:, PB:PB + 768])
    qb_ref[:, 0:256] = (b[:, 0:256] * QK_SCALE).astype(BF16)
    qb_ref[:, 256:768] = b[:, 256:768].astype(BF16)
    c = _dot(xn, w_ref[:, PC:PC + 512])
    qc_ref[:, 0:256] = (_rope(_head_norm(c[:, 0:256], nq_ref[...]), cos, sin) * QK_SCALE).astype(BF16)
    qc_ref[:, 256:384] = _rope(_head_norm(c[:, 256:384], nk_ref[...]), cos, sin).astype(BF16)
    qc_ref[:, 384:512] = c[:, 384:512].astype(BF16)
    z_ref[...] = _dot(xn, w_ref[:, PZ:PZ + 256])
    xbc_ref[...] = _dot(xn, w_ref[:, PX:PX + 768])
    dt_ref[...] = _dot(xn, w_ref[:, PDT:PDT + DT_PAD])


def _proj(x_all, mod_l, gain, w_proj, layer, rope_tab, nq, nk, nlat):
    nb, seq, _ = x_all.shape
    tok = lambda w: pl.BlockSpec((None, PROJ_TM, w), lambda b, t: (b, t, 0))
    outs = [(512, BF16), (768, BF16), (512, BF16), (256, F32), (768, F32), (DT_PAD, F32)]
    return pl.pallas_call(
        _proj_kernel,
        out_shape=[jax.ShapeDtypeStruct((nb, seq, w), dt) for w, dt in outs],
        grid=(nb, seq // PROJ_TM),
        in_specs=[
            tok(D_MODEL), _mod_spec(3), _mod_spec(4),
            _const_spec((1, D_MODEL)),
            _layer_spec((D_MODEL, PROJ_N), layer),
            pl.BlockSpec((None, PROJ_TM, 512), lambda b, t: (b // nlat, t, 0)),
            _const_spec((1, 256)), _const_spec((1, 256)),
        ],
        out_specs=[tok(w) for w, _ in outs],
        compiler_params=_cparams(("arbitrary", "arbitrary")),
        name="in_proj",
    )(x_all, mod_l, mod_l, gain.reshape(1, D_MODEL), w_proj, rope_tab, nq, nk)


def _gqa_core(q, keys, values, masks, sink_ref, t):
    lo = _lane_iota((1, LANE)) < HEAD_DIM
    t0, t1 = q[:, 0:LANE], q[:, LANE:2 * LANE]
    zero = jnp.zeros_like(t0)
    outs = []
    for hk in range(2):
        if hk == 0:
            qs = jnp.concatenate([jnp.where(lo, t0, zero), jnp.where(lo, t1, zero)], axis=0)
        else:
            qs = jnp.concatenate([jnp.where(lo, zero, t0), jnp.where(lo, zero, t1)], axis=0)
        scores = []
        for k, m in zip(keys, masks):
            s = _dot_nt(qs, k)
            scores.append(s if m is None else jnp.where(m, s, NEG_INF))
        extra = None
        if sink_ref is not None:
            row = lax.broadcasted_iota(jnp.int32, (2 * t, 1), 0)
            extra = jnp.where(row < t, sink_ref[2 * hk], sink_ref[2 * hk + 1])
        outs.append(_softmax_pv(scores, values, extra))
    o0, o1 = outs
    return jnp.concatenate([jnp.where(lo, o0[0:t], o1[0:t]), jnp.where(lo, o0[t:2 * t], o1[t:2 * t])], axis=1)


def _softmax_pv(scores, values, extra_logit=None):
    m = None
    for s in scores:
        sm = jnp.max(s, axis=-1, keepdims=True)
        m = sm if m is None else jnp.maximum(m, sm)
    if extra_logit is not None:
        m = jnp.maximum(m, extra_logit)
    den = None
    acc = None
    for s, v in zip(scores, values):
        p = jnp.exp(s - m)
        ps = jnp.sum(p, axis=-1, keepdims=True)
        den = ps if den is None else den + ps
        pv = _dot(p.astype(BF16), v)
        acc = pv if acc is None else acc + pv
    if extra_logit is not None:
        den = den + jnp.exp(extra_logit - m)
    return acc * (1.0 / den)


ATT_TQ = 128


def _gqa_kernel(*refs, window, seq):
    if window:
        sink_ref, q_ref, k_ref, v_ref, kc_ref, vc_ref, o_ref = refs
    else:
        q_ref, k_ref, v_ref, kc_ref, vc_ref, o_ref = refs
    tq = ATT_TQ
    if window:
        nkw = tq + 2 * WA_WINDOW
        i = pl.program_id(1)
        start = pl.multiple_of(jnp.clip(i * tq - WA_WINDOW, 0, seq - nkw), LANE)
        kw = k_ref[pl.ds(start, nkw), :]
        vw = v_ref[pl.ds(start, nkw), :]
        qpos = i * tq + lax.broadcasted_iota(jnp.int32, (2 * tq, nkw), 0) % tq
        kpos = start + lax.broadcasted_iota(jnp.int32, (2 * tq, nkw), 1)
        mask = jnp.abs(qpos - kpos) <= WA_WINDOW
        sink = sink_ref
    else:
        kw, vw, mask, sink = k_ref[...], v_ref[...], None, None
    o = _gqa_core(q_ref[...], [kw, kc_ref[...]], [vw, vc_ref[...]], [mask, None], sink, tq)
    o_ref[...] = o.astype(BF16)


def _gqa_attention(qkv, sink, nlat, window):
    nb, seq, _ = qkv.shape
    in_specs = [
        pl.BlockSpec((None, ATT_TQ, 256), lambda b, i: (b, i, 0)),
        pl.BlockSpec((None, seq, LANE), lambda b, i: (b, 0, 2)),
        pl.BlockSpec((None, seq, LANE), lambda b, i: (b, 0, 3)),
        pl.BlockSpec((None, CTX_LEN, LANE), lambda b, i: (nlat, b, 2)),
        pl.BlockSpec((None, CTX_LEN, LANE), lambda b, i: (nlat, b, 3)),
    ]
    args = [qkv, qkv, qkv, qkv, qkv]
    if window:
        in_specs.insert(0, pl.BlockSpec(memory_space=pltpu.SMEM))
        args.insert(0, sink)
    return pl.pallas_call(
        functools.partial(_gqa_kernel, window=window, seq=seq),
        out_shape=jax.ShapeDtypeStruct((nb, seq, BRANCH_W), BF16),
        grid=(nlat, seq // ATT_TQ),
        in_specs=in_specs,
        out_specs=pl.BlockSpec((None, ATT_TQ, BRANCH_W), lambda b, i: (b, i, 0)),
        compiler_params=_cparams(("arbitrary", "arbitrary")),
        name="window_attention" if window else "dense_attention",
    )(*args)


NA_KROWS = NA_ROWS * GRID_W


NA_STEP_ROWS = 4


def _na_kernel(q_ref, k_ref, v_ref, kc_ref, vc_ref, bias_ref, o_ref, *, rows):
    lo = _lane_iota((1, LANE)) < HEAD_DIM
    for i in range(NA_STEP_ROWS):
        r = pl.program_id(1) * NA_STEP_ROWS + i
        rs = jnp.clip(r - NA_ROWS // 2, 0, rows - NA_ROWS)
        pat = rs - r + (NA_ROWS - 1)
        start = pl.multiple_of(rs * GRID_W, GRID_W)
        qrows = slice(i * GRID_W, (i + 1) * GRID_W)
        outs = []
        for blk in range(2):
            cols = slice(blk * LANE, (blk + 1) * LANE)
            qb = q_ref[qrows, cols]
            zero = jnp.zeros_like(qb)
            qs = jnp.concatenate([jnp.where(lo, qb, zero), jnp.where(lo, zero, qb)], axis=0)
            kw = k_ref[pl.ds(start, NA_KROWS), cols]
            vw = v_ref[pl.ds(start, NA_KROWS), cols]
            s_nb = _dot_nt(qs, kw) + bias_ref[pat, blk]
            s_ctx = _dot_nt(qs, kc_ref[:, cols])
            o = _softmax_pv([s_nb, s_ctx], [vw, vc_ref[:, cols]])
            outs.append(jnp.where(lo, o[0:GRID_W], o[GRID_W:2 * GRID_W]))
        o_ref[qrows, :] = jnp.concatenate(outs, axis=1).astype(BF16)


def _na_bias_table(rpb):
    c = jnp.arange(GRID_W)
    qstart = jnp.clip(c - NA_COLS // 2, 0, GRID_W - NA_COLS)
    ok = (c[None, :] >= qstart[:, None]) & (c[None, :] < qstart[:, None] + NA_COLS)
    dc = jnp.clip(c[None, :] - c[:, None] + NA_COLS - 1, 0, 2 * NA_COLS - 2)
    band = jnp.where(ok, rpb[..., dc], NEG_INF)
    t = jnp.stack([band[:, :, p:p + NA_ROWS] for p in range(NA_ROWS)], axis=1)
    t = t.transpose(0, 1, 2, 4, 3, 5).reshape(rpb.shape[0], NA_ROWS, 2, 2 * GRID_W, NA_KROWS)
    return t.astype(F32)


def _na_attention(qkv, bias, layer, nlat):
    nb, seq, _ = qkv.shape
    rows = seq // GRID_W
    tq = NA_STEP_ROWS * GRID_W
    return pl.pallas_call(
        functools.partial(_na_kernel, rows=rows),
        out_shape=jax.ShapeDtypeStruct((nb, seq, BRANCH_W), BF16),
        grid=(nlat, rows // NA_STEP_ROWS),
        in_specs=[
            pl.BlockSpec((None, tq, 256), lambda b, r: (b, r, 0)),
            pl.BlockSpec((None, seq, 256), lambda b, r: (b, 0, 1)),
            pl.BlockSpec((None, seq, 256), lambda b, r: (b, 0, 2)),
            pl.BlockSpec((None, CTX_LEN, 256), lambda b, r: (nlat, b, 1)),
            pl.BlockSpec((None, CTX_LEN, 256), lambda b, r: (nlat, b, 2)),
            _layer_spec(bias.shape[1:], layer),
        ],
        out_specs=pl.BlockSpec((None, tq, BRANCH_W), lambda b, r: (b, r, 0)),
        compiler_params=_cparams(("arbitrary", "arbitrary")),
        name="neighborhood_attention",
    )(qkv, qkv, qkv, qkv, qkv, bias)


def _ctx_kernel(sink_ref, qa_ref, ka_ref, va_ref, qb_ref, kb_ref, vb_ref, qc_ref, kc_ref, vc_ref, yd_ref,
                _a, _b, _c, _d, oa_ref, ob_ref, oc_ref, od_ref):
    t = CTX_LEN
    oa_ref[...] = _gqa_core(qa_ref[...], [ka_ref[...]], [va_ref[...]], [None], sink_ref, t).astype(BF16)
    oc_ref[...] = _gqa_core(qc_ref[...], [kc_ref[...]], [vc_ref[...]], [None], None, t).astype(BF16)
    lo = _lane_iota((1, LANE)) < HEAD_DIM
    outs = []
    for blk in range(2):
        cols = slice(blk * LANE, (blk + 1) * LANE)
        qb = qb_ref[:, cols]
        zero = jnp.zeros_like(qb)
        qs = jnp.concatenate([jnp.where(lo, qb, zero), jnp.where(lo, zero, qb)], axis=0)
        o = _softmax_pv([_dot_nt(qs, kb_ref[:, cols])], [vb_ref[:, cols]])
        outs.append(jnp.where(lo, o[0:t], o[t:2 * t]))
    ob_ref[...] = jnp.concatenate(outs, axis=1).astype(BF16)
    od_ref[...] = yd_ref[...]


def _ctx_attention(sink, qa, qb, qc, yd_ctx, ya, yb, yc, yd, nlat):
    blk = lambda w, j: pl.BlockSpec((None, CTX_LEN, w), lambda b: (nlat, b, j))
    anyspec = pl.BlockSpec(memory_space=pl.ANY)
    out_spec = pl.BlockSpec((None, CTX_LEN, BRANCH_W), lambda b: (nlat, b, 0))
    return pl.pallas_call(
        _ctx_kernel,
        out_shape=[jax.ShapeDtypeStruct(ya.shape, BF16)] * 4,
        grid=(nlat,),
        in_specs=[
            pl.BlockSpec(memory_space=pltpu.SMEM),
            blk(256, 0), blk(LANE, 2), blk(LANE, 3),
            blk(256, 0), blk(256, 1), blk(256, 2),
            blk(256, 0), blk(LANE, 2), blk(LANE, 3),
            pl.BlockSpec((None, CTX_LEN, BRANCH_W), lambda b: (b, 0, 0)),
            anyspec, anyspec, anyspec, anyspec,
        ],
        out_specs=[out_spec] * 4,
        input_output_aliases={11: 0, 12: 1, 13: 2, 14: 3},
        compiler_params=_cparams(("arbitrary",)),
        name="context_attention",
    )(sink, qa, qa, qa, qb, qb, qb, qc, qc, qc, yd_ctx, ya, yb, yc, yd)


CONV_PAD = 8
FIN_ROWS = 256


def _split3(t):
    hi = t.astype(BF16)
    r1 = t - hi.astype(F32)
    mid = r1.astype(BF16)
    return hi, mid, (r1 - mid.astype(F32)).astype(BF16)


HEAD_ROWS = 16
CONV_PHASES = 4
N_SLAB = SSM_CONV_CH // LANE


def _ssd_kernel(xbc_ref, xbcc_ref, dt_ref, dtc_ref, z_ref, zc_ref, cw_ref, cb_ref, dtb_ref, alog_ref,
                dvec_ref, nrm_ref, y_ref, yc_ref, pad_ref, u_ref, ct_ref, dt8_ref, w_ref, e_ref,
                yacc_ref, loc_ref, ce_ref, *, seq):
    ltot = CTX_LEN + seq
    nchunk = ltot // SSM_CHUNK
    nctx = CTX_LEN // SSM_CHUNK
    half = SSM_CONV // 2
    cs = SSM_CHUNK
    hr = HEAD_ROWS

    def conv_into(src_ref, n, dst):
        for sl in range(N_SLAB):
            pad_ref[sl, 0:CONV_PAD, :] = jnp.zeros((CONV_PAD, LANE), F32)
            pad_ref[sl, CONV_PAD + n:2 * CONV_PAD + n, :] = jnp.zeros((CONV_PAD, LANE), F32)
            pad_ref[sl, CONV_PAD:CONV_PAD + n, :] = src_ref[:, sl * LANE:(sl + 1) * LANE]
        q = FIN_ROWS // CONV_PHASES

        def block(i, carry):
            r0 = pl.multiple_of(i * FIN_ROWS, FIN_ROWS)
            for sl in range(N_SLAB):
                lanes = slice(sl * LANE, (sl + 1) * LANE)
                for p in range(CONV_PHASES):
                    acc = jnp.broadcast_to(cb_ref[:, lanes], (q, LANE))
                    for k in range(SSM_CONV):
                        tap = pad_ref[sl, pl.ds(r0 + CONV_PAD + p + k - half, q, stride=CONV_PHASES), :]
                        acc = acc + tap * cw_ref[k:k + 1, lanes]
                    u_ref[sl, pl.ds(dst + r0 + p, q, stride=CONV_PHASES), :] = _silu(acc)
            return carry

        lax.fori_loop(0, n // FIN_ROWS, block, 0)

    conv_into(xbcc_ref, CTX_LEN, 0)
    conv_into(xbc_ref, seq, CTX_LEN)

    row = lax.broadcasted_iota(jnp.int32, (cs, cs), 0)
    col = lax.broadcasted_iota(jnp.int32, (cs, cs), 1)
    causal = (col <= row, col >= row)
    lo = _lane_iota((1, LANE)) < SSM_STATE // 2

    def softplus(t):
        return jnp.maximum(t, 0.0) + jnp.log1p(jnp.exp(-jnp.abs(t)))

    sel = jnp.where(lax.broadcasted_iota(jnp.int32, (hr, LANE), 0) == lax.broadcasted_iota(jnp.int32, (hr, LANE), 1),
                    1.0, 0.0).astype(BF16)

    def heads_to_rows(src_ref):
        return sum(_dot_nt(sel, part) for part in _split3(src_ref[...]))

    dt_t = softplus(jnp.concatenate([heads_to_rows(dtc_ref), heads_to_rows(dt_ref)], axis=1) + dtb_ref[...])
    da_t = dt_t * (-jnp.exp(alog_ref[...]))

    def by_chunk(t):
        return jnp.concatenate([t[:, c * cs:(c + 1) * cs] for c in range(nchunk)], axis=0)

    dt_c, da_c = by_chunk(dt_t), by_chunk(da_t)
    tri = jnp.where(row <= col, 1.0, 0.0).astype(BF16)
    pre = sum(_dot(part, tri) for part in _split3(da_c))
    is_fwd = lax.broadcasted_iota(jnp.int32, (nchunk * hr, 1), 0) % hr < SSM_HEADS
    ct = jnp.where(is_fwd, pre, (pre[:, cs - 1:cs] - pre) + da_c)
    tot = jnp.where(is_fwd, ct[:, cs - 1:cs], ct[:, 0:1])
    ct_ref[...] = ct
    dt8_ref[...] = dt_c
    w_ref[...] = dt_c * jnp.exp(tot - ct)
    e_ref[...] = jnp.broadcast_to(jnp.exp(tot), (nchunk * hr, cs))

    def chunk_rows(c):
        return pl.ds(pl.multiple_of(c * cs, cs), cs)

    def head_rows(c):
        return pl.ds(pl.multiple_of(c * hr, hr), hr)

    def phase1(c, carry):
        rows = chunk_rows(c)
        ct8 = ct_ref[head_rows(c), :]
        dt8 = dt8_ref[head_rows(c), :]
        w8 = w_ref[head_rows(c), :]
        for g in range(SSM_GROUPS):
            gl = slice(g * LANE, (g + 1) * LANE)
            xg = u_ref[g, rows, :]
            bg = u_ref[SSM_GROUPS + g, rows, :]
            cg = u_ref[2 * SSM_GROUPS + g, rows, :]
            xcat = jnp.concatenate([jnp.where(lo, xg, 0.0), jnp.where(lo, 0.0, xg)], axis=0).astype(BF16)
            bt = bg.T
            cb = _dot_nt(cg.astype(BF16), bg.astype(BF16))
            mh = [None, None]
            ces, bts = [], []
            for d in range(2):
                for hh in range(2):
                    j = d * SSM_HEADS + 2 * g + hh
                    cum_j = jnp.broadcast_to(ct8[j:j + 1, :], (cs, cs)).T
                    seg = jnp.where(causal[d], cum_j - ct8[j:j + 1, :], NEG_INF)
                    term = jnp.exp(seg) * dt8[j:j + 1, :]
                    mh[hh] = term if d == 0 else mh[hh] + term
                    ces.append((cg * jnp.exp(cum_j)).astype(BF16))
                    bts.append((bt * w8[j:j + 1, :]).astype(BF16))
            m = jnp.concatenate([(cb * mh[0]).astype(BF16), (cb * mh[1]).astype(BF16)], axis=1)
            yacc_ref[rows, gl] = _dot(m, xcat)
            ce_ref[rows, g * 4 * LANE:(g + 1) * 4 * LANE] = jnp.concatenate(ces, axis=1)
            for d in range(2):
                st = _dot(jnp.concatenate([bts[2 * d], bts[2 * d + 1]], axis=1), xcat)
                loc_ref[c, :, d * SSM_INNER + g * LANE:d * SSM_INNER + (g + 1) * LANE] = st
        return carry

    lax.fori_loop(0, nchunk, phase1, 0, unroll=2)

    def scan(order, d):
        cols = slice(d * SSM_INNER, (d + 1) * SSM_INNER)

        def body(t, s):
            c = order(t)
            local = loc_ref[c, :, cols]
            loc_ref[c, :, cols] = s
            e8 = e_ref[head_rows(c), :]
            j0 = d * SSM_HEADS
            decay = jnp.concatenate([jnp.where(lo, e8[j0 + 2 * g:j0 + 2 * g + 1, :], e8[j0 + 2 * g + 1:j0 + 2 * g + 2, :])
                                     for g in range(SSM_GROUPS)], axis=1)
            return s * decay + local
        lax.fori_loop(0, nchunk, body, jnp.zeros((SSM_STATE, SSM_INNER), F32))

    scan(lambda t: t, 0)
    scan(lambda t: jnp.where(t < nctx, nctx - 1 - t, nchunk + nctx - 1 - t), 1)

    def finish_chunk(c, z_rows, out_ref, out_rows):
        rows = chunk_rows(c)
        s_in = loc_ref[c]
        ys = []
        for g in range(SSM_GROUPS):
            gl = slice(g * LANE, (g + 1) * LANE)
            sf = s_in[:, gl]
            sb = s_in[:, SSM_INNER + g * LANE:SSM_INNER + (g + 1) * LANE]
            scat = jnp.concatenate([jnp.where(lo, sf, 0.0), jnp.where(lo, 0.0, sf),
                                    jnp.where(lo, sb, 0.0), jnp.where(lo, 0.0, sb)], axis=0).astype(BF16)
            ys.append(yacc_ref[rows, gl] + _dot(ce_ref[rows, g * 4 * LANE:(g + 1) * 4 * LANE], scat))
        xs = jnp.concatenate([u_ref[g, rows, :] for g in range(SSM_GROUPS)], axis=1)
        y = jnp.concatenate(ys, axis=1) + dvec_ref[...] * xs
        y = y * _silu(z_rows)
        out_ref[out_rows, :] = _rms(y, nrm_ref[...]).astype(BF16)

    for c in range(nctx):
        finish_chunk(c, zc_ref[c * cs:(c + 1) * cs, :], yc_ref, slice(c * cs, (c + 1) * cs))

    def finish_body(cl, carry):
        out_rows = chunk_rows(cl)
        finish_chunk(cl + nctx, z_ref[out_rows, :], y_ref, out_rows)
        return carry

    lax.fori_loop(0, nchunk - nctx, finish_body, 0, unroll=2)


def _ssd(xbc, dt, z, conv_w, conv_b, dt_bias, a_log, dvec, nrm, nlat):
    nb, seq, _ = xbc.shape
    ltot = CTX_LEN + seq
    nchunk = ltot // SSM_CHUNK
    lat = lambda w: pl.BlockSpec((None, seq, w), lambda b: (b, 0, 0))
    ctx = lambda w: pl.BlockSpec((None, CTX_LEN, w), lambda b: (nlat, b, 0))
    return pl.pallas_call(
        functools.partial(_ssd_kernel, seq=seq),
        out_shape=[jax.ShapeDtypeStruct((nb, seq, SSM_INNER), BF16),
                   jax.ShapeDtypeStruct((nlat, CTX_LEN, SSM_INNER), BF16)],
        grid=(nlat,),
        in_specs=[
            lat(SSM_CONV_CH), ctx(SSM_CONV_CH), lat(DT_PAD), ctx(DT_PAD), lat(SSM_INNER), ctx(SSM_INNER),
            _const_spec((8, SSM_CONV_CH)), _const_spec((1, SSM_CONV_CH)),
            _const_spec((HEAD_ROWS, 1)), _const_spec((HEAD_ROWS, 1)),
            _const_spec((1, SSM_INNER)), _const_spec((1, SSM_INNER)),
        ],
        out_specs=[pl.BlockSpec((None, seq, SSM_INNER), lambda b: (b, 0, 0)),
                   pl.BlockSpec((None, CTX_LEN, SSM_INNER), lambda b: (b, 0, 0))],
        scratch_shapes=[
            pltpu.VMEM((N_SLAB, seq + 2 * CONV_PAD, LANE), F32),
            pltpu.VMEM((N_SLAB, ltot, LANE), F32),
            pltpu.VMEM((nchunk * HEAD_ROWS, SSM_CHUNK), F32),
            pltpu.VMEM((nchunk * HEAD_ROWS, SSM_CHUNK), F32),
            pltpu.VMEM((nchunk * HEAD_ROWS, SSM_CHUNK), F32),
            pltpu.VMEM((nchunk * HEAD_ROWS, SSM_CHUNK), F32),
            pltpu.VMEM((ltot, SSM_INNER), F32),
            pltpu.VMEM((nchunk, SSM_STATE, 2 * SSM_INNER), F32),
            pltpu.VMEM((ltot, 2 * SSM_GROUPS * 2 * SSM_STATE), BF16),
        ],
        compiler_params=_cparams(("arbitrary",)),
        name="ssd",
    )(xbc, xbc, dt, dt, z, z, conv_w, conv_b, dt_bias, a_log, dvec, nrm)


MERGE_TM = 512


def _merge_kernel(x_ref, sh_ref, sc_ref, gt_ref, g_ref, ya_ref, yb_ref, yc_ref, yd_ref,
                  wgl_ref, bg_ref, wbr_ref, wo_ref, o_ref):
    x = x_ref[...]
    xn = _ada(x, g_ref[...], sh_ref[...], sc_ref[...]).astype(BF16)
    mix = None
    for k, y_ref in enumerate((ya_ref, yb_ref, yc_ref, yd_ref)):
        cols = slice(k * D_MODEL, (k + 1) * D_MODEL)
        pre = _dot(xn, wgl_ref[:, cols]) + bg_ref[:, cols]
        gate = 1.0 / (1.0 + jnp.exp(-pre))
        term = gate * _dot(y_ref[...], wbr_ref[k])
        mix = term if mix is None else mix + term
    o_ref[...] = x + gt_ref[...] * _dot(mix.astype(BF16), wo_ref[...])


def _merge(x_all, mod_l, gain, ys, w_gate, b_gate, w_branch, w_out, layer, nb):
    seq = x_all.shape[1]
    tok = lambda w: pl.BlockSpec((None, MERGE_TM, w), lambda b, t: (b, t, 0))
    return pl.pallas_call(
        _merge_kernel,
        out_shape=jax.ShapeDtypeStruct((nb, seq, D_MODEL), F32),
        grid=(nb, seq // MERGE_TM),
        in_specs=[
            tok(D_MODEL), _mod_spec(3), _mod_spec(4), _mod_spec(5),
            _const_spec((1, D_MODEL)),
            tok(BRANCH_W), tok(BRANCH_W), tok(BRANCH_W), tok(BRANCH_W),
            _layer_spec((D_MODEL, N_BRANCH * D_MODEL), layer), _layer_spec((1, N_BRANCH * D_MODEL), layer),
            _layer_spec((N_BRANCH, BRANCH_W, D_MODEL), layer), _layer_spec((D_MODEL, D_MODEL), layer),
        ],
        out_specs=tok(D_MODEL),
        compiler_params=_cparams(("arbitrary", "arbitrary")),
        name="merge",
    )(x_all, mod_l, mod_l, mod_l, gain.reshape(1, D_MODEL), *ys,
      w_gate, b_gate.reshape(DEPTH, 1, N_BRANCH * D_MODEL), w_branch, w_out)


def _rope_table(seq):
    pos = jnp.arange(seq)
    axes = jnp.stack([pos // GRID_W, pos % GRID_W], axis=-1).astype(F32)
    quarter = HEAD_DIM // 4
    inv = 1.0 / (ROPE_THETA ** (jnp.arange(quarter, dtype=F32) * 4.0 / HEAD_DIM))
    ang = axes[:, :, None] * inv
    cos, sin = jnp.cos(ang), jnp.sin(ang)
    cos_h = jnp.concatenate([cos, cos], axis=-1).reshape(seq, HEAD_DIM)
    sin_h = jnp.concatenate([-sin, sin], axis=-1).reshape(seq, HEAD_DIM)
    lat = jnp.concatenate([jnp.tile(cos_h, (1, 4)), jnp.tile(sin_h, (1, 4))], axis=-1)
    ident = jnp.concatenate([jnp.ones((seq, 256), F32), jnp.zeros((seq, 256), F32)], axis=-1)
    return jnp.stack([lat, ident])


def _permute_heads(w, axis):
    parts = jnp.split(w, 4, axis=axis)
    return jnp.concatenate([parts[j] for j in Q_HEAD_ORDER], axis=axis)


CAST_SPLIT = 4


def _cast_kernel(w_ref, o_ref):
    o_ref[...] = w_ref[...].astype(BF16)


def _cast_bf16(w):
    depth, r, c = w.shape
    tr = r // CAST_SPLIT
    spec = pl.BlockSpec((None, tr, c), lambda l, i: (l, i, 0))
    return pl.pallas_call(
        _cast_kernel,
        out_shape=jax.ShapeDtypeStruct(w.shape, BF16),
        grid=(depth, CAST_SPLIT),
        in_specs=[spec], out_specs=spec,
        compiler_params=_cparams(("arbitrary", "arbitrary")),
        name="cast_bf16",
    )(w)


def _pack_proj_kernel(w_ref, o_ref):
    lo = _lane_iota((1, LANE)) < HEAD_DIM
    o_ref[...] = w_ref[...].astype(BF16)
    for base in (PA, PC):
        t0 = w_ref[:, base:base + LANE]
        t1 = w_ref[:, base + LANE:base + 2 * LANE]
        o_ref[:, base:base + LANE] = jnp.where(lo, t0, pltpu.roll(t1, HEAD_DIM, 1)).astype(BF16)
        o_ref[:, base + LANE:base + 2 * LANE] = jnp.where(lo, pltpu.roll(t0, HEAD_DIM, 1), t1).astype(BF16)


def _pack_proj(w_in):
    depth, r, _ = w_in.shape
    tr = r // CAST_SPLIT
    return pl.pallas_call(
        _pack_proj_kernel,
        out_shape=jax.ShapeDtypeStruct((depth, r, PROJ_N), BF16),
        grid=(depth, CAST_SPLIT),
        in_specs=[pl.BlockSpec((None, tr, PROJ_N), lambda l, i: (l, i, 0))],
        out_specs=pl.BlockSpec((None, tr, PROJ_N), lambda l, i: (l, i, 0)),
        compiler_params=_cparams(("arbitrary", "arbitrary")),
        name="pack_proj",
    )(w_in)


def kernel(x, c, ctx, c_ctx, w_mod, b_mod, norm_ffn1, ffn1_w_gate, ffn1_w_up, ffn1_w_down, norm_mix, w_in, b_gate,
           attn_sink, na_rpb, qk_norm_q, qk_norm_k, ssm_conv_w, ssm_conv_b, ssm_dt_bias, ssm_a_log, ssm_d, ssm_norm,
           w_branch, w_out, norm_ffn2, ffn2_w_gate, ffn2_w_up, ffn2_w_down, final_norm):
    nlat, seq, _ = x.shape
    assert nlat * CTX_LEN == seq and nlat + 1 <= MOD_ROWS
    nb = nlat + 1
    x_all = jnp.concatenate([x, ctx.reshape(1, seq, D_MODEL)], axis=0)
    c_rows = jnp.concatenate([c, c_ctx[None], jnp.zeros((MOD_ROWS - nb, D_MODEL), F32)], axis=0)
    mod = _modulation(c_rows, w_mod, b_mod).reshape(DEPTH, N_MOD, MOD_ROWS, 1, D_MODEL)
    rope_tab = _rope_table(seq)
    head_col = lambda v: jnp.pad(v.reshape(-1, 1), ((0, HEAD_ROWS - v.size), (0, 0)))

    ffn1_w = [_cast_bf16(w) for w in (ffn1_w_gate, ffn1_w_up, ffn1_w_down)]
    ffn2_w = [_cast_bf16(w) for w in (ffn2_w_gate, ffn2_w_up, ffn2_w_down)]
    w_proj = _pack_proj(w_in)
    w_gl = _cast_bf16(w_in[:, :, PDT + 2 * SSM_HEADS:])
    w_o = _cast_bf16(w_out)
    wbr = jnp.stack([_permute_heads(w_branch[:, 0], 1), w_branch[:, 1],
                     _permute_heads(w_branch[:, 2], 1), w_branch[:, 3]], axis=1).astype(BF16)
    na_bias = _na_bias_table(na_rpb)

    for l in range(DEPTH):
        with_ctx = l < DEPTH - 1
        last = l == DEPTH - 1
        nq = jnp.tile(qk_norm_q[l], 4).reshape(1, 256)
        nk = jnp.tile(qk_norm_k[l], 4).reshape(1, 256)
        conv_w = jnp.pad(ssm_conv_w[l], ((0, 8 - SSM_CONV), (0, 0)))
        dvec = jnp.repeat(ssm_d[l], SSM_INNER // SSM_HEADS).reshape(1, SSM_INNER)

        x_all = _ffn(x_all, mod[l], (0, 1, 2), norm_ffn1[l], *ffn1_w, l, nb)
        qa, qb, qc, z, xbc, dt = _proj(x_all, mod[l], norm_mix[l], w_proj, l, rope_tab, nq, nk, nlat)
        ya = _gqa_attention(qa, attn_sink[l], nlat, window=True)
        yb = _na_attention(qb, na_bias, l, nlat)
        yc = _gqa_attention(qc, None, nlat, window=False)
        yd, yd_ctx = _ssd(xbc, dt, z, conv_w, ssm_conv_b[l].reshape(1, -1), head_col(ssm_dt_bias[l]),
                          head_col(ssm_a_log[l]), dvec, ssm_norm[l].reshape(1, -1), nlat)
        nbm = nb if with_ctx else nlat
        if with_ctx:
            ya, yb, yc, yd = _ctx_attention(attn_sink[l], qa, qb, qc, yd_ctx, ya, yb, yc, yd, nlat)
        x_all = _merge(x_all, mod[l], norm_mix[l], (ya, yb, yc, yd), w_gl, b_gate, wbr, w_o, l, nbm)
        x_all = _ffn(x_all, mod[l], (6, 7, 8), norm_ffn2[l], *ffn2_w, l, nbm,
                     final_gain=final_norm if last else None)
    return x_all
```

```python
import functools

import jax
import jax.numpy as jnp
from jax import lax
from jax.experimental import pallas as pl
from jax.experimental.pallas import tpu as pltpu

D_MODEL = 1024
DEPTH = 4
CTX_LEN = 256
GRID_W = 64
HEAD_DIM = 64
ROPE_THETA = 10000.0
EPS = 1e-6
NEG_INF = -1e30
D_FF = 2816
N_BRANCH = 4
BRANCH_W = 256
WA_WINDOW = 128
NA_ROWS = 8
NA_COLS = 16
SSM_HEADS = 4
SSM_GROUPS = 2
SSM_STATE = 128
SSM_CONV = 5
SSM_CHUNK = 128
SSM_INNER = 256
SSM_CONV_CH = 768
N_MOD = 9
MOD_ROWS = 16
LANE = 128
DT_PAD = LANE
QK_SCALE = HEAD_DIM ** -0.5
VMEM_LIMIT = 56 * 1024 * 1024

BF16 = jnp.bfloat16
F32 = jnp.float32


def _cparams(sem):
    return pltpu.CompilerParams(dimension_semantics=sem, vmem_limit_bytes=VMEM_LIMIT)


def _const_spec(shape):
    nd = len(shape)
    return pl.BlockSpec(shape, lambda *_: (0,) * nd)


def _layer_spec(shape, layer):
    nd = len(shape)
    return pl.BlockSpec((None,) + tuple(shape), lambda *_: (layer,) + (0,) * nd)


def _dot(a, b):
    return jnp.dot(a, b, preferred_element_type=F32)


def _dot_nt(a, b):
    return lax.dot_general(a, b, (((1,), (1,)), ((), ())), preferred_element_type=F32)


def _silu(t):
    return t * (1.0 / (1.0 + jnp.exp(-t)))


def _rms(x, g):
    return (x * lax.rsqrt(jnp.mean(x * x, axis=-1, keepdims=True) + EPS)) * g


def _ada(x, g, shift, scale):
    return _rms(x, g) * (1.0 + scale) + shift


def _mod_kernel(c_ref, w_ref, b_ref, o_ref):
    act = _silu(c_ref[...]).astype(BF16)
    o_ref[...] = _dot(act, w_ref[...].astype(BF16)) + b_ref[...]


def _modulation(c_rows, w_mod, b_mod):
    return pl.pallas_call(
        _mod_kernel,
        out_shape=jax.ShapeDtypeStruct((DEPTH, N_MOD, MOD_ROWS, D_MODEL), F32),
        grid=(DEPTH, N_MOD),
        in_specs=[
            pl.BlockSpec((MOD_ROWS, D_MODEL), lambda l, i: (0, 0)),
            pl.BlockSpec((None, D_MODEL, D_MODEL), lambda l, i: (l, 0, i)),
            pl.BlockSpec((None, 1, D_MODEL), lambda l, i: (l, 0, i)),
        ],
        out_specs=pl.BlockSpec((None, None, MOD_ROWS, D_MODEL), lambda l, i: (l, i, 0, 0)),
        compiler_params=_cparams(("arbitrary", "arbitrary")),
        name="modulation",
    )(c_rows, w_mod, b_mod.reshape(DEPTH, 1, N_MOD * D_MODEL))


def _mod_spec(kind):
    return pl.BlockSpec((None, None, 1, D_MODEL), lambda b, t: (kind, b, 0, 0))


FFN_TM = 1024
FFN_CHUNKS = ((0, 1024), (1024, 2048), (2048, 2816))


def _ffn_kernel(*refs, final, nlat_split):
    refs = list(refs)
    x_ref = refs.pop(0)
    xc_ref = refs.pop(0) if nlat_split is not None else None
    sh_ref, sc_ref, gt_ref, g_ref, wg_ref, wu_ref, wd_ref = refs[:7]
    fg_ref = refs[7] if final else None
    o_ref = refs[-1]
    x = x_ref[...]
    if nlat_split is not None:
        x = jnp.where(pl.program_id(0) == nlat_split, xc_ref[...], x)
    xn = _ada(x, g_ref[...], sh_ref[...], sc_ref[...]).astype(BF16)
    acc = None
    for c0, c1 in FFN_CHUNKS:
        gate = _dot(xn, wg_ref[:, c0:c1])
        up = _dot(xn, wu_ref[:, c0:c1])
        h = (_silu(gate) * up).astype(BF16)
        part = _dot(h, wd_ref[c0:c1, :])
        acc = part if acc is None else acc + part
    y = x + (0.5 * gt_ref[...]) * acc
    if final:
        y = _rms(y, fg_ref[...])
    o_ref[...] = y


def _ffn(x_all, mod_l, kinds, gain, wg, wu, wd, layer, nb, final_gain=None, x_ctx=None):
    seq = x_all.shape[1]
    final = final_gain is not None
    tok = pl.BlockSpec((None, FFN_TM, D_MODEL), lambda b, t: (b, t, 0))
    if x_ctx is None:
        nlat_split = None
        in_specs, args = [tok], [x_all]
    else:
        nlat_split = x_all.shape[0]
        in_specs = [pl.BlockSpec((None, FFN_TM, D_MODEL), lambda b, t: (jnp.minimum(b, nlat_split - 1), t, 0)),
                    pl.BlockSpec((None, FFN_TM, D_MODEL), lambda b, t: (0, t, 0))]
        args = [x_all, x_ctx]
    in_specs += [
        _mod_spec(kinds[0]), _mod_spec(kinds[1]), _mod_spec(kinds[2]),
        _const_spec((1, D_MODEL)),
        _layer_spec((D_MODEL, D_FF), layer), _layer_spec((D_MODEL, D_FF), layer),
        _layer_spec((D_FF, D_MODEL), layer),
    ]
    args += [mod_l, mod_l, mod_l, gain.reshape(1, D_MODEL), wg, wu, wd]
    if final:
        in_specs.append(_const_spec((1, D_MODEL)))
        args.append(final_gain.reshape(1, D_MODEL))
    return pl.pallas_call(
        functools.partial(_ffn_kernel, final=final, nlat_split=nlat_split),
        out_shape=jax.ShapeDtypeStruct((nb, seq, D_MODEL), F32),
        grid=(nb, seq // FFN_TM),
        in_specs=in_specs,
        out_specs=pl.BlockSpec((None, FFN_TM, D_MODEL), lambda b, t: (b, t, 0)),
        compiler_params=_cparams(("arbitrary", "arbitrary")),
        name="ffn",
    )(*args)


PA, PB, PC, PZ, PX, PDT = 0, 512, 1280, 1792, 2048, 2816
PROJ_N = PDT + DT_PAD
PROJ_TM = 512
Q_HEAD_ORDER = (0, 2, 1, 3)


def _lane_iota(shape):
    return lax.broadcasted_iota(jnp.int32, shape, len(shape) - 1)


def _rope(t, cos, sin):
    w = t.shape[-1]
    first = (_lane_iota((1, w)) % 32) < 16
    rot = jnp.where(first, pltpu.roll(t, w - 16, 1), pltpu.roll(t, 16, 1))
    return t * cos[:, :w] + rot * sin[:, :w]


def _head_norm(t, gain):
    w = t.shape[-1]
    lane = _lane_iota((1, w))
    sq = t * t
    scale = jnp.zeros_like(t)
    for h in range(w // HEAD_DIM):
        m = (lane >= h * HEAD_DIM) & (lane < (h + 1) * HEAD_DIM)
        ms = jnp.sum(jnp.where(m, sq, 0.0), axis=-1, keepdims=True) * (1.0 / HEAD_DIM)
        scale = jnp.where(m, lax.rsqrt(ms + EPS), scale)
    return (t * scale) * gain[:, :w]


def _pair_heads(t):
    lo = _lane_iota((1, LANE)) < HEAD_DIM
    t0, t1 = t[:, 0:LANE], t[:, LANE:2 * LANE]
    return jnp.concatenate([jnp.where(lo, t0, pltpu.roll(t1, HEAD_DIM, 1)),
                            jnp.where(lo, pltpu.roll(t0, HEAD_DIM, 1), t1)], axis=1)


def _proj_kernel(x_ref, sh_ref, sc_ref, g_ref, w_ref, rope_ref, nq_ref, nk_ref,
                 qa_ref, qb_ref, qc_ref, z_ref, xbc_ref, dt_ref):
    xn = _ada(x_ref[...], g_ref[...], sh_ref[...], sc_ref[...]).astype(BF16)
    cos = rope_ref[:, 0:256]
    sin = rope_ref[:, 256:512]
    a = _dot(xn, w_ref[:, PA:PA + 512])
    qa_ref[:, 0:256] = _pair_heads(_rope(a[:, 0:256], cos, sin) * QK_SCALE).astype(BF16)
    qa_ref[:, 256:384] = _rope(a[:, 256:384], cos, sin).astype(BF16)
    qa_ref[:, 384:512] = a[:, 384:512].astype(BF16)
    b = _dot(xn, w_ref[:, PB:PB + 768])
    qb_ref[:, 0:256] = (b[:, 0:256] * QK_SCALE).astype(BF16)
    qb_ref[:, 256:768] = b[:, 256:768].astype(BF16)
    c = _dot(xn, w_ref[:, PC:PC + 512])
    qc_ref[:, 0:256] = _pair_heads(_rope(_head_norm(c[:, 0:256], nq_ref[...]), cos, sin) * QK_SCALE).astype(BF16)
    qc_ref[:, 256:384] = _rope(_head_norm(c[:, 256:384], nk_ref[...]), cos, sin).astype(BF16)
    qc_ref[:, 384:512] = c[:, 384:512].astype(BF16)
    z_ref[...] = _dot(xn, w_ref[:, PZ:PZ + 256])
    xbc_ref[...] = _dot(xn, w_ref[:, PX:PX + 768])
    dt_ref[...] = _dot(xn, w_ref[:, PDT:PDT + DT_PAD])


def _proj(x_all, mod_l, gain, w_proj, layer, rope_tab, nq, nk, nlat):
    nb, seq, _ = x_all.shape
    tok = lambda w: pl.BlockSpec((None, PROJ_TM, w), lambda b, t: (b, t, 0))
    outs = [(512, BF16), (768, BF16), (512, BF16), (256, F32), (768, F32), (DT_PAD, F32)]
    return pl.pallas_call(
        _proj_kernel,
        out_shape=[jax.ShapeDtypeStruct((nb, seq, w), dt) for w, dt in outs],
        grid=(nb, seq // PROJ_TM),
        in_specs=[
            tok(D_MODEL), _mod_spec(3), _mod_spec(4),
            _const_spec((1, D_MODEL)),
            _layer_spec((D_MODEL, PROJ_N), layer),
            pl.BlockSpec((None, PROJ_TM, 512), lambda b, t: (b // nlat, t, 0)),
            _const_spec((1, 256)), _const_spec((1, 256)),
        ],
        out_specs=[tok(w) for w, _ in outs],
        compiler_params=_cparams(("arbitrary", "arbitrary")),
        name="in_proj",
    )(x_all, mod_l, mod_l, gain.reshape(1, D_MODEL), w_proj, rope_tab, nq, nk)


def _gqa_core(q, keys, values, masks, sink_ref, t):
    lo = _lane_iota((1, LANE)) < HEAD_DIM
    t0, t1 = q[:, 0:LANE], q[:, LANE:2 * LANE]
    zero = jnp.zeros_like(t0)
    outs = []
    for hk in range(2):
        if hk == 0:
            qs = jnp.concatenate([jnp.where(lo, t0, zero), jnp.where(lo, t1, zero)], axis=0)
        else:
            qs = jnp.concatenate([jnp.where(lo, zero, t0), jnp.where(lo, zero, t1)], axis=0)
        scores = []
        for k, m in zip(keys, masks):
            s = _dot_nt(qs, k)
            scores.append(s if m is None else jnp.where(m, s, NEG_INF))
        extra = None
        if sink_ref is not None:
            row = lax.broadcasted_iota(jnp.int32, (2 * t, 1), 0)
            extra = jnp.where(row < t, sink_ref[2 * hk], sink_ref[2 * hk + 1])
        outs.append(_softmax_pv(scores, values, extra))
    o0, o1 = outs
    return jnp.concatenate([jnp.where(lo, o0[0:t], o1[0:t]), jnp.where(lo, o0[t:2 * t], o1[t:2 * t])], axis=1)


def _softmax_pv(scores, values, extra_logit=None):
    m = None
    for s in scores:
        sm = jnp.max(s, axis=-1, keepdims=True)
        m = sm if m is None else jnp.maximum(m, sm)
    if extra_logit is not None:
        m = jnp.maximum(m, extra_logit)
    den = None
    acc = None
    for s, v in zip(scores, values):
        p = jnp.exp(s - m)
        ps = jnp.sum(p, axis=-1, keepdims=True)
        den = ps if den is None else den + ps
        pv = _dot(p.astype(BF16), v)
        acc = pv if acc is None else acc + pv
    if extra_logit is not None:
        den = den + jnp.exp(extra_logit - m)
    return acc * (1.0 / den)


WINDOW_TQ = 256
DENSE_TQ = 128


def _gqa_kernel(*refs, window, seq, tq):
    if window:
        sink_ref, q_ref, k_ref, v_ref, kc_ref, vc_ref, o_ref = refs
    else:
        q_ref, k_ref, v_ref, kc_ref, vc_ref, o_ref = refs
    if window:
        nkw = tq + 2 * WA_WINDOW
        i = pl.program_id(1)
        start = pl.multiple_of(jnp.clip(i * tq - WA_WINDOW, 0, seq - nkw), LANE)
        kw = k_ref[pl.ds(start, nkw), :]
        vw = v_ref[pl.ds(start, nkw), :]
        qpos = i * tq + lax.broadcasted_iota(jnp.int32, (2 * tq, nkw), 0) % tq
        kpos = start + lax.broadcasted_iota(jnp.int32, (2 * tq, nkw), 1)
        mask = jnp.abs(qpos - kpos) <= WA_WINDOW
        sink = sink_ref
    else:
        kw, vw, mask, sink = k_ref[...], v_ref[...], None, None
    o = _gqa_core(q_ref[...], [kw, kc_ref[...]], [vw, vc_ref[...]], [mask, None], sink, tq)
    o_ref[...] = o.astype(BF16)


def _gqa_attention(qkv, sink, nlat, window):
    nb, seq, _ = qkv.shape
    tq = WINDOW_TQ if window else DENSE_TQ
    in_specs = [
        pl.BlockSpec((None, tq, 256), lambda b, i: (b, i, 0)),
        pl.BlockSpec((None, seq, LANE), lambda b, i: (b, 0, 2)),
        pl.BlockSpec((None, seq, LANE), lambda b, i: (b, 0, 3)),
        pl.BlockSpec((None, CTX_LEN, LANE), lambda b, i: (nlat, b, 2)),
        pl.BlockSpec((None, CTX_LEN, LANE), lambda b, i: (nlat, b, 3)),
    ]
    args = [qkv, qkv, qkv, qkv, qkv]
    if window:
        in_specs.insert(0, pl.BlockSpec(memory_space=pltpu.SMEM))
        args.insert(0, sink)
    return pl.pallas_call(
        functools.partial(_gqa_kernel, window=window, seq=seq, tq=tq),
        out_shape=jax.ShapeDtypeStruct((nlat, seq, BRANCH_W), BF16),
        grid=(nlat, seq // tq),
        in_specs=in_specs,
        out_specs=pl.BlockSpec((None, tq, BRANCH_W), lambda b, i: (b, i, 0)),
        compiler_params=_cparams(("arbitrary", "arbitrary")),
        name="window_attention" if window else "dense_attention",
    )(*args)


NA_KROWS = NA_ROWS * GRID_W


NA_STEP_ROWS = 4


def _na_kernel(q_ref, k_ref, v_ref, kc_ref, vc_ref, bias_ref, o_ref, *, rows):
    lo = _lane_iota((1, LANE)) < HEAD_DIM
    for i in range(NA_STEP_ROWS):
        r = pl.program_id(1) * NA_STEP_ROWS + i
        rs = jnp.clip(r - NA_ROWS // 2, 0, rows - NA_ROWS)
        pat = rs - r + (NA_ROWS - 1)
        start = pl.multiple_of(rs * GRID_W, GRID_W)
        qrows = slice(i * GRID_W, (i + 1) * GRID_W)
        outs = []
        for blk in range(2):
            cols = slice(blk * LANE, (blk + 1) * LANE)
            qb = q_ref[qrows, cols]
            zero = jnp.zeros_like(qb)
            qs = jnp.concatenate([jnp.where(lo, qb, zero), jnp.where(lo, zero, qb)], axis=0)
            kw = k_ref[pl.ds(start, NA_KROWS), cols]
            vw = v_ref[pl.ds(start, NA_KROWS), cols]
            s_nb = _dot_nt(qs, kw) + bias_ref[pat, blk]
            s_ctx = _dot_nt(qs, kc_ref[:, cols])
            o = _softmax_pv([s_nb, s_ctx], [vw, vc_ref[:, cols]])
            outs.append(jnp.where(lo, o[0:GRID_W], o[GRID_W:2 * GRID_W]))
        o_ref[qrows, :] = jnp.concatenate(outs, axis=1).astype(BF16)


def _na_attention(qkv, bias, layer, nlat):
    nb, seq, _ = qkv.shape
    rows = seq // GRID_W
    tq = NA_STEP_ROWS * GRID_W
    return pl.pallas_call(
        functools.partial(_na_kernel, rows=rows),
        out_shape=jax.ShapeDtypeStruct((nlat, seq, BRANCH_W), BF16),
        grid=(nlat, rows // NA_STEP_ROWS),
        in_specs=[
            pl.BlockSpec((None, tq, 256), lambda b, r: (b, r, 0)),
            pl.BlockSpec((None, seq, 256), lambda b, r: (b, 0, 1)),
            pl.BlockSpec((None, seq, 256), lambda b, r: (b, 0, 2)),
            pl.BlockSpec((None, CTX_LEN, 256), lambda b, r: (nlat, b, 1)),
            pl.BlockSpec((None, CTX_LEN, 256), lambda b, r: (nlat, b, 2)),
            _layer_spec(bias.shape[1:], layer),
        ],
        out_specs=pl.BlockSpec((None, tq, BRANCH_W), lambda b, r: (b, r, 0)),
        compiler_params=_cparams(("arbitrary", "arbitrary")),
        name="neighborhood_attention",
    )(qkv, qkv, qkv, qkv, qkv, bias)


def _ctx_kernel(sink_ref, qa_ref, ka_ref, va_ref, qb_ref, kb_ref, vb_ref, qc_ref, kc_ref, vc_ref,
                oa_ref, ob_ref, oc_ref):
    t = CTX_LEN
    oa_ref[...] = _gqa_core(qa_ref[...], [ka_ref[...]], [va_ref[...]], [None], sink_ref, t).astype(BF16)
    oc_ref[...] = _gqa_core(qc_ref[...], [kc_ref[...]], [vc_ref[...]], [None], None, t).astype(BF16)
    lo = _lane_iota((1, LANE)) < HEAD_DIM
    outs = []
    for blk in range(2):
        cols = slice(blk * LANE, (blk + 1) * LANE)
        qb = qb_ref[:, cols]
        zero = jnp.zeros_like(qb)
        qs = jnp.concatenate([jnp.where(lo, qb, zero), jnp.where(lo, zero, qb)], axis=0)
        o = _softmax_pv([_dot_nt(qs, kb_ref[:, cols])], [vb_ref[:, cols]])
        outs.append(jnp.where(lo, o[0:t], o[t:2 * t]))
    ob_ref[...] = jnp.concatenate(outs, axis=1).astype(BF16)


def _ctx_attention(sink, qa, qb, qc, nlat):
    blk = lambda w, j: pl.BlockSpec((None, CTX_LEN, w), lambda b: (nlat, b, j))
    out_spec = pl.BlockSpec((None, CTX_LEN, BRANCH_W), lambda b: (0, b, 0))
    return pl.pallas_call(
        _ctx_kernel,
        out_shape=[jax.ShapeDtypeStruct((1, nlat * CTX_LEN, BRANCH_W), BF16)] * 3,
        grid=(nlat,),
        in_specs=[
            pl.BlockSpec(memory_space=pltpu.SMEM),
            blk(256, 0), blk(LANE, 2), blk(LANE, 3),
            blk(256, 0), blk(256, 1), blk(256, 2),
            blk(256, 0), blk(LANE, 2), blk(LANE, 3),
        ],
        out_specs=[out_spec] * 3,
        compiler_params=_cparams(("arbitrary",)),
        name="context_attention",
    )(sink, qa, qa, qa, qb, qb, qb, qc, qc, qc)


CONV_PAD = 8
FIN_ROWS = 256


def _split3(t):
    hi = t.astype(BF16)
    r1 = t - hi.astype(F32)
    mid = r1.astype(BF16)
    return hi, mid, (r1 - mid.astype(F32)).astype(BF16)


HEAD_ROWS = 16
CONV_PHASES = 4
N_SLAB = SSM_CONV_CH // LANE


def _ssd_kernel(xbc_ref, xbcc_ref, dt_ref, dtc_ref, z_ref, zc_ref, cw_ref, cb_ref, dtb_ref, alog_ref,
                dvec_ref, nrm_ref, y_ref, yc_ref, pad_ref, u_ref, ct_ref, dt8_ref, w_ref, e_ref,
                yacc_ref, loc_ref, ce_ref, *, seq):
    ltot = CTX_LEN + seq
    nchunk = ltot // SSM_CHUNK
    nctx = CTX_LEN // SSM_CHUNK
    half = SSM_CONV // 2
    cs = SSM_CHUNK
    hr = HEAD_ROWS

    def conv_into(src_ref, n, dst):
        for sl in range(N_SLAB):
            pad_ref[sl, 0:CONV_PAD, :] = jnp.zeros((CONV_PAD, LANE), F32)
            pad_ref[sl, CONV_PAD + n:2 * CONV_PAD + n, :] = jnp.zeros((CONV_PAD, LANE), F32)
            pad_ref[sl, CONV_PAD:CONV_PAD + n, :] = src_ref[:, sl * LANE:(sl + 1) * LANE]
        q = FIN_ROWS // CONV_PHASES

        def block(i, carry):
            r0 = pl.multiple_of(i * FIN_ROWS, FIN_ROWS)
            for sl in range(N_SLAB):
                lanes = slice(sl * LANE, (sl + 1) * LANE)
                for p in range(CONV_PHASES):
                    acc = jnp.broadcast_to(cb_ref[:, lanes], (q, LANE))
                    for k in range(SSM_CONV):
                        tap = pad_ref[sl, pl.ds(r0 + CONV_PAD + p + k - half, q, stride=CONV_PHASES), :]
                        acc = acc + tap * cw_ref[k:k + 1, lanes]
                    u_ref[sl, pl.ds(dst + r0 + p, q, stride=CONV_PHASES), :] = _silu(acc)
            return carry

        lax.fori_loop(0, n // FIN_ROWS, block, 0)

    conv_into(xbcc_ref, CTX_LEN, 0)
    conv_into(xbc_ref, seq, CTX_LEN)

    row = lax.broadcasted_iota(jnp.int32, (cs, cs), 0)
    col = lax.broadcasted_iota(jnp.int32, (cs, cs), 1)
    causal = (col <= row, col >= row)
    lo = _lane_iota((1, LANE)) < SSM_STATE // 2

    def softplus(t):
        return jnp.maximum(t, 0.0) + jnp.log1p(jnp.exp(-jnp.abs(t)))

    sel = jnp.where(lax.broadcasted_iota(jnp.int32, (hr, LANE), 0) == lax.broadcasted_iota(jnp.int32, (hr, LANE), 1),
                    1.0, 0.0).astype(BF16)

    def heads_to_rows(src_ref):
        return sum(_dot_nt(sel, part) for part in _split3(src_ref[...]))

    dt_t = softplus(jnp.concatenate([heads_to_rows(dtc_ref), heads_to_rows(dt_ref)], axis=1) + dtb_ref[...])
    da_t = dt_t * (-jnp.exp(alog_ref[...]))

    def by_chunk(t):
        return jnp.concatenate([t[:, c * cs:(c + 1) * cs] for c in range(nchunk)], axis=0)

    dt_c, da_c = by_chunk(dt_t), by_chunk(da_t)
    tri = jnp.where(row <= col, 1.0, 0.0).astype(BF16)
    pre = sum(_dot(part, tri) for part in _split3(da_c))
    is_fwd = lax.broadcasted_iota(jnp.int32, (nchunk * hr, 1), 0) % hr < SSM_HEADS
    ct = jnp.where(is_fwd, pre, (pre[:, cs - 1:cs] - pre) + da_c)
    tot = jnp.where(is_fwd, ct[:, cs - 1:cs], ct[:, 0:1])
    ct_ref[...] = ct
    dt8_ref[...] = dt_c
    w_ref[...] = dt_c * jnp.exp(tot - ct)
    e_ref[...] = jnp.broadcast_to(jnp.exp(tot), (nchunk * hr, cs))

    def chunk_rows(c):
        return pl.ds(pl.multiple_of(c * cs, cs), cs)

    def head_rows(c):
        return pl.ds(pl.multiple_of(c * hr, hr), hr)

    def phase1(c, carry):
        rows = chunk_rows(c)
        ct8 = ct_ref[head_rows(c), :]
        dt8 = dt8_ref[head_rows(c), :]
        w8 = w_ref[head_rows(c), :]
        for g in range(SSM_GROUPS):
            gl = slice(g * LANE, (g + 1) * LANE)
            xg = u_ref[g, rows, :]
            bg = u_ref[SSM_GROUPS + g, rows, :]
            cg = u_ref[2 * SSM_GROUPS + g, rows, :]
            xcat = jnp.concatenate([jnp.where(lo, xg, 0.0), jnp.where(lo, 0.0, xg)], axis=0).astype(BF16)
            bt = bg.T
            cb = _dot_nt(cg.astype(BF16), bg.astype(BF16))
            mh = [None, None]
            ces, bts = [], []
            for d in range(2):
                for hh in range(2):
                    j = d * SSM_HEADS + 2 * g + hh
                    cum_j = jnp.broadcast_to(ct8[j:j + 1, :], (cs, cs)).T
                    seg = jnp.where(causal[d], cum_j - ct8[j:j + 1, :], NEG_INF)
                    term = jnp.exp(seg) * dt8[j:j + 1, :]
                    mh[hh] = term if d == 0 else mh[hh] + term
                    ces.append((cg * jnp.exp(cum_j)).astype(BF16))
                    bts.append((bt * w8[j:j + 1, :]).astype(BF16))
            m = jnp.concatenate([(cb * mh[0]).astype(BF16), (cb * mh[1]).astype(BF16)], axis=1)
            yacc_ref[rows, gl] = _dot(m, xcat)
            ce_ref[rows, g * 4 * LANE:(g + 1) * 4 * LANE] = jnp.concatenate(ces, axis=1)
            for d in range(2):
                st = _dot(jnp.concatenate([bts[2 * d], bts[2 * d + 1]], axis=1), xcat)
                loc_ref[c, :, d * SSM_INNER + g * LANE:d * SSM_INNER + (g + 1) * LANE] = st
        return carry

    lax.fori_loop(0, nchunk, phase1, 0, unroll=2)

    def scan(order, d):
        cols = slice(d * SSM_INNER, (d + 1) * SSM_INNER)

        def body(t, s):
            c = order(t)
            local = loc_ref[c, :, cols]
            loc_ref[c, :, cols] = s
            e8 = e_ref[head_rows(c), :]
            j0 = d * SSM_HEADS
            decay = jnp.concatenate([jnp.where(lo, e8[j0 + 2 * g:j0 + 2 * g + 1, :], e8[j0 + 2 * g + 1:j0 + 2 * g + 2, :])
                                     for g in range(SSM_GROUPS)], axis=1)
            return s * decay + local
        lax.fori_loop(0, nchunk, body, jnp.zeros((SSM_STATE, SSM_INNER), F32))

    scan(lambda t: t, 0)
    scan(lambda t: jnp.where(t < nctx, nctx - 1 - t, nchunk + nctx - 1 - t), 1)

    def finish_chunk(c, z_rows, out_ref, out_rows):
        rows = chunk_rows(c)
        s_in = loc_ref[c]
        ys = []
        for g in range(SSM_GROUPS):
            gl = slice(g * LANE, (g + 1) * LANE)
            sf = s_in[:, gl]
            sb = s_in[:, SSM_INNER + g * LANE:SSM_INNER + (g + 1) * LANE]
            scat = jnp.concatenate([jnp.where(lo, sf, 0.0), jnp.where(lo, 0.0, sf),
                                    jnp.where(lo, sb, 0.0), jnp.where(lo, 0.0, sb)], axis=0).astype(BF16)
            ys.append(yacc_ref[rows, gl] + _dot(ce_ref[rows, g * 4 * LANE:(g + 1) * 4 * LANE], scat))
        xs = jnp.concatenate([u_ref[g, rows, :] for g in range(SSM_GROUPS)], axis=1)
        y = jnp.concatenate(ys, axis=1) + dvec_ref[...] * xs
        y = y * _silu(z_rows)
        out_ref[out_rows, :] = _rms(y, nrm_ref[...]).astype(BF16)

    for c in range(nctx):
        finish_chunk(c, zc_ref[c * cs:(c + 1) * cs, :], yc_ref, slice(c * cs, (c + 1) * cs))

    def finish_body(cl, carry):
        out_rows = chunk_rows(cl)
        finish_chunk(cl + nctx, z_ref[out_rows, :], y_ref, out_rows)
        return carry

    lax.fori_loop(0, nchunk - nctx, finish_body, 0, unroll=2)


def _ssd(xbc, dt, z, conv_w, conv_b, dt_bias, a_log, dvec, nrm, nlat):
    nb, seq, _ = xbc.shape
    ltot = CTX_LEN + seq
    nchunk = ltot // SSM_CHUNK
    lat = lambda w: pl.BlockSpec((None, seq, w), lambda b: (b, 0, 0))
    ctx = lambda w: pl.BlockSpec((None, CTX_LEN, w), lambda b: (nlat, b, 0))
    return pl.pallas_call(
        functools.partial(_ssd_kernel, seq=seq),
        out_shape=[jax.ShapeDtypeStruct((nlat, seq, SSM_INNER), BF16),
                   jax.ShapeDtypeStruct((nlat, CTX_LEN, SSM_INNER), BF16)],
        grid=(nlat,),
        in_specs=[
            lat(SSM_CONV_CH), ctx(SSM_CONV_CH), lat(DT_PAD), ctx(DT_PAD), lat(SSM_INNER), ctx(SSM_INNER),
            _const_spec((8, SSM_CONV_CH)), _const_spec((1, SSM_CONV_CH)),
            _const_spec((HEAD_ROWS, 1)), _const_spec((HEAD_ROWS, 1)),
            _const_spec((1, SSM_INNER)), _const_spec((1, SSM_INNER)),
        ],
        out_specs=[pl.BlockSpec((None, seq, SSM_INNER), lambda b: (b, 0, 0)),
                   pl.BlockSpec((None, CTX_LEN, SSM_INNER), lambda b: (b, 0, 0))],
        scratch_shapes=[
            pltpu.VMEM((N_SLAB, seq + 2 * CONV_PAD, LANE), F32),
            pltpu.VMEM((N_SLAB, ltot, LANE), F32),
            pltpu.VMEM((nchunk * HEAD_ROWS, SSM_CHUNK), F32),
            pltpu.VMEM((nchunk * HEAD_ROWS, SSM_CHUNK), F32),
            pltpu.VMEM((nchunk * HEAD_ROWS, SSM_CHUNK), F32),
            pltpu.VMEM((nchunk * HEAD_ROWS, SSM_CHUNK), F32),
            pltpu.VMEM((ltot, SSM_INNER), F32),
            pltpu.VMEM((nchunk, SSM_STATE, 2 * SSM_INNER), F32),
            pltpu.VMEM((ltot, 2 * SSM_GROUPS * 2 * SSM_STATE), BF16),
        ],
        compiler_params=_cparams(("arbitrary",)),
        name="ssd",
    )(xbc, xbc, dt, dt, z, z, conv_w, conv_b, dt_bias, a_log, dvec, nrm)


MERGE_TM = 512


def _merge_kernel(*refs, nlat_split):
    x_ref, sh_ref, sc_ref, gt_ref, g_ref = refs[:5]
    y_refs = refs[5:9]
    yc_refs = refs[9:13] if nlat_split is not None else None
    wgl_ref, bg_ref, wbr_ref, wo_ref, o_ref = refs[-5:]
    x = x_ref[...]
    xn = _ada(x, g_ref[...], sh_ref[...], sc_ref[...]).astype(BF16)
    mix = None
    for k in range(N_BRANCH):
        cols = slice(k * D_MODEL, (k + 1) * D_MODEL)
        pre = _dot(xn, wgl_ref[:, cols]) + bg_ref[:, cols]
        gate = 1.0 / (1.0 + jnp.exp(-pre))
        y = y_refs[k][...]
        if nlat_split is not None:
            y = jnp.where(pl.program_id(0) == nlat_split, yc_refs[k][...], y)
        term = gate * _dot(y, wbr_ref[k])
        mix = term if mix is None else mix + term
    o_ref[...] = x + gt_ref[...] * _dot(mix.astype(BF16), wo_ref[...])


def _merge(x_all, mod_l, gain, ys, ys_ctx, w_gate, b_gate, w_branch, w_out, layer, nb):
    seq = x_all.shape[1]
    nlat = ys[0].shape[0]
    tok = lambda w: pl.BlockSpec((None, MERGE_TM, w), lambda b, t: (b, t, 0))
    lat = pl.BlockSpec((None, MERGE_TM, BRANCH_W), lambda b, t: (jnp.minimum(b, nlat - 1), t, 0))
    in_specs = [tok(D_MODEL), _mod_spec(3), _mod_spec(4), _mod_spec(5), _const_spec((1, D_MODEL))] + [lat] * 4
    args = [x_all, mod_l, mod_l, mod_l, gain.reshape(1, D_MODEL), *ys]
    if ys_ctx is not None:
        in_specs += [pl.BlockSpec((None, MERGE_TM, BRANCH_W), lambda b, t: (0, t, 0))] * 4
        args += list(ys_ctx)
    in_specs += [
        _layer_spec((D_MODEL, N_BRANCH * D_MODEL), layer), _layer_spec((1, N_BRANCH * D_MODEL), layer),
        _layer_spec((N_BRANCH, BRANCH_W, D_MODEL), layer), _layer_spec((D_MODEL, D_MODEL), layer),
    ]
    args += [w_gate, b_gate.reshape(DEPTH, 1, N_BRANCH * D_MODEL), w_branch, w_out]
    return pl.pallas_call(
        functools.partial(_merge_kernel, nlat_split=nlat if ys_ctx is not None else None),
        out_shape=jax.ShapeDtypeStruct((nb, seq, D_MODEL), F32),
        grid=(nb, seq // MERGE_TM),
        in_specs=in_specs,
        out_specs=tok(D_MODEL),
        compiler_params=_cparams(("arbitrary", "arbitrary")),
        name="merge",
    )(*args)


def _rope_table(seq):
    pos = jnp.arange(seq)
    axes = jnp.stack([pos // GRID_W, pos % GRID_W], axis=-1).astype(F32)
    quarter = HEAD_DIM // 4
    inv = 1.0 / (ROPE_THETA ** (jnp.arange(quarter, dtype=F32) * 4.0 / HEAD_DIM))
    ang = axes[:, :, None] * inv
    cos, sin = jnp.cos(ang), jnp.sin(ang)
    cos_h = jnp.concatenate([cos, cos], axis=-1).reshape(seq, HEAD_DIM)
    sin_h = jnp.concatenate([-sin, sin], axis=-1).reshape(seq, HEAD_DIM)
    lat = jnp.concatenate([jnp.tile(cos_h, (1, 4)), jnp.tile(sin_h, (1, 4))], axis=-1)
    ident = jnp.concatenate([jnp.ones((seq, 256), F32), jnp.zeros((seq, 256), F32)], axis=-1)
    return jnp.stack([lat, ident])


def _permute_heads(w, axis):
    parts = jnp.split(w, 4, axis=axis)
    return jnp.concatenate([parts[j] for j in Q_HEAD_ORDER], axis=axis)


CAST_SPLIT = 4


def _cast_kernel(w_ref, o_ref):
    o_ref[...] = w_ref[...].astype(BF16)


def _cast_bf16(w):
    depth, r, c = w.shape
    tr = r // CAST_SPLIT
    spec = pl.BlockSpec((None, tr, c), lambda l, i: (l, i, 0))
    return pl.pallas_call(
        _cast_kernel,
        out_shape=jax.ShapeDtypeStruct(w.shape, BF16),
        grid=(depth, CAST_SPLIT),
        in_specs=[spec], out_specs=spec,
        compiler_params=_cparams(("arbitrary", "arbitrary")),
        name="cast_bf16",
    )(w)


MXU_TILE = 256
GATE_COL0 = PDT + 2 * SSM_HEADS


def _transpose_cast_kernel(w_ref, o_ref):
    eye = jnp.where(lax.broadcasted_iota(jnp.int32, (MXU_TILE, MXU_TILE), 0)
                    == lax.broadcasted_iota(jnp.int32, (MXU_TILE, MXU_TILE), 1), 1.0, 0.0).astype(BF16)
    for k0 in range(0, D_MODEL, MXU_TILE):
        blk = w_ref[:, k0:k0 + MXU_TILE].astype(BF16)
        o_ref[k0:k0 + MXU_TILE, :] = _dot_nt(eye, blk).astype(BF16)


def _pack_w_in(w_in_t):
    depth = w_in_t.shape[0]
    proj = pl.pallas_call(
        _transpose_cast_kernel,
        out_shape=jax.ShapeDtypeStruct((depth, D_MODEL, PROJ_N), BF16),
        grid=(depth,),
        in_specs=[pl.BlockSpec((None, PROJ_N, D_MODEL), lambda l: (l, 0, 0))],
        out_specs=pl.BlockSpec((None, D_MODEL, PROJ_N), lambda l: (l, 0, 0)),
        compiler_params=_cparams(("arbitrary",)),
        name="pack_proj",
    )(w_in_t)
    gate = pl.pallas_call(
        lambda w_ref, o_ref: _transpose_cast_kernel(w_ref.at[0], o_ref),
        out_shape=jax.ShapeDtypeStruct((depth, D_MODEL, N_BRANCH * D_MODEL), BF16),
        grid=(depth, N_BRANCH),
        in_specs=[pl.BlockSpec((pl.Element(1), pl.Element(D_MODEL), pl.Element(D_MODEL)),
                               lambda l, k: (l, pl.multiple_of(GATE_COL0 + k * D_MODEL, 8), 0))],
        out_specs=pl.BlockSpec((None, D_MODEL, D_MODEL), lambda l, k: (l, 0, k)),
        compiler_params=_cparams(("arbitrary", "arbitrary")),
        name="pack_gate",
    )(w_in_t)
    return proj, gate


NA_BANDS = 2 * NA_ROWS - 1


def _na_bias_kernel(rpb_ref, o_ref):
    c = lax.broadcasted_iota(jnp.int32, (GRID_W, LANE), 0)
    kc = lax.broadcasted_iota(jnp.int32, (GRID_W, LANE), 1)
    qstart = jnp.clip(c - NA_COLS // 2, 0, GRID_W - NA_COLS)
    ok = (kc >= qstart) & (kc < qstart + NA_COLS)
    lo = kc < GRID_W
    for h in range(2 * 2):
        blk, g = divmod(h, 2)
        bands = []
        for dr in range(NA_BANDS):
            rows = jnp.broadcast_to(rpb_ref[h, dr:dr + 1, :], (GRID_W, LANE))
            skew = pltpu.roll(rows, LANE - (NA_COLS - 1), 1, stride=1, stride_axis=0)
            bands.append(jnp.where(ok, skew, NEG_INF))
        for p in range(NA_ROWS):
            for m in range(NA_ROWS // 2):
                pair = jnp.where(lo, bands[p + 2 * m], pltpu.roll(bands[p + 2 * m + 1], GRID_W, 1))
                o_ref[p, blk, g * GRID_W:(g + 1) * GRID_W, m * LANE:(m + 1) * LANE] = pair


def _na_bias_table(rpb):
    depth, heads = rpb.shape[:2]
    padded = jnp.pad(rpb, ((0, 0), (0, 0), (0, 16 - rpb.shape[2]), (0, LANE - rpb.shape[3])))
    return pl.pallas_call(
        _na_bias_kernel,
        out_shape=jax.ShapeDtypeStruct((depth, NA_ROWS, 2, 2 * GRID_W, NA_KROWS), F32),
        grid=(depth,),
        in_specs=[pl.BlockSpec((None, heads, 16, LANE), lambda l: (l, 0, 0, 0))],
        out_specs=pl.BlockSpec((None, NA_ROWS, 2, 2 * GRID_W, NA_KROWS), lambda l: (l, 0, 0, 0, 0)),
        compiler_params=_cparams(("arbitrary",)),
        name="na_bias",
    )(padded)


def kernel(x, c, ctx, c_ctx, w_mod, b_mod, norm_ffn1, ffn1_w_gate, ffn1_w_up, ffn1_w_down, norm_mix, w_in, b_gate,
           attn_sink, na_rpb, qk_norm_q, qk_norm_k, ssm_conv_w, ssm_conv_b, ssm_dt_bias, ssm_a_log, ssm_d, ssm_norm,
           w_branch, w_out, norm_ffn2, ffn2_w_gate, ffn2_w_up, ffn2_w_down, final_norm):
    nlat, seq, _ = x.shape
    assert nlat * CTX_LEN == seq and nlat + 1 <= MOD_ROWS
    nb = nlat + 1
    c_rows = jnp.concatenate([c, c_ctx[None], jnp.zeros((MOD_ROWS - nb, D_MODEL), F32)], axis=0)
    mod = _modulation(c_rows, w_mod, b_mod).reshape(DEPTH, N_MOD, MOD_ROWS, 1, D_MODEL)
    rope_tab = _rope_table(seq)
    head_col = lambda v: jnp.pad(v.reshape(-1, 1), ((0, HEAD_ROWS - v.size), (0, 0)))

    ffn1_w = [_cast_bf16(w) for w in (ffn1_w_gate, ffn1_w_up, ffn1_w_down)]
    ffn2_w = [_cast_bf16(w) for w in (ffn2_w_gate, ffn2_w_up, ffn2_w_down)]
    w_proj, w_gl = _pack_w_in(jnp.swapaxes(w_in, 1, 2))
    w_o = _cast_bf16(w_out)
    wbr = jnp.stack([_permute_heads(w_branch[:, 0], 1), w_branch[:, 1],
                     _permute_heads(w_branch[:, 2], 1), w_branch[:, 3]], axis=1).astype(BF16)
    na_bias = _na_bias_table(na_rpb)

    for l in range(DEPTH):
        with_ctx = l < DEPTH - 1
        last = l == DEPTH - 1
        nq = jnp.tile(qk_norm_q[l], 4).reshape(1, 256)
        nk = jnp.tile(qk_norm_k[l], 4).reshape(1, 256)
        conv_w = jnp.pad(ssm_conv_w[l], ((0, 8 - SSM_CONV), (0, 0)))
        dvec = jnp.repeat(ssm_d[l], SSM_INNER // SSM_HEADS).reshape(1, SSM_INNER)

        if l == 0:
            x_all = _ffn(x, mod[l], (0, 1, 2), norm_ffn1[l], *ffn1_w, l, nb, x_ctx=ctx.reshape(1, seq, D_MODEL))
        else:
            x_all = _ffn(x_all, mod[l], (0, 1, 2), norm_ffn1[l], *ffn1_w, l, nb)
        qa, qb, qc, z, xbc, dt = _proj(x_all, mod[l], norm_mix[l], w_proj, l, rope_tab, nq, nk, nlat)
        ya = _gqa_attention(qa, attn_sink[l], nlat, window=True)
        yb = _na_attention(qb, na_bias, l, nlat)
        yc = _gqa_attention(qc, None, nlat, window=False)
        yd, yd_ctx = _ssd(xbc, dt, z, conv_w, ssm_conv_b[l].reshape(1, -1), head_col(ssm_dt_bias[l]),
                          head_col(ssm_a_log[l]), dvec, ssm_norm[l].reshape(1, -1), nlat)
        nbm = nb if with_ctx else nlat
        ys_ctx = None
        if with_ctx:
            ys_ctx = (*_ctx_attention(attn_sink[l], qa, qb, qc, nlat), yd_ctx.reshape(1, seq, SSM_INNER))
        x_all = _merge(x_all, mod[l], norm_mix[l], (ya, yb, yc, yd), ys_ctx, w_gl, b_gate, wbr, w_o, l, nbm)
        x_all = _ffn(x_all, mod[l], (6, 7, 8), norm_ffn2[l], *ffn2_w, l, nbm,
                     final_gain=final_norm if last else None)
    return x_all
```

```python
import functools

import jax
import jax.numpy as jnp
from jax import lax
from jax.experimental import pallas as pl
from jax.experimental.pallas import tpu as pltpu

D_MODEL = 1024
DEPTH = 4
CTX_LEN = 256
GRID_W = 64
HEAD_DIM = 64
ROPE_THETA = 10000.0
EPS = 1e-6
NEG_INF = -1e30
D_FF = 2816
N_BRANCH = 4
BRANCH_W = 256
WA_WINDOW = 128
NA_ROWS = 8
NA_COLS = 16
SSM_HEADS = 4
SSM_GROUPS = 2
SSM_STATE = 128
SSM_CONV = 5
SSM_CHUNK = 128
SSM_INNER = 256
SSM_CONV_CH = 768
N_MOD = 9
MOD_ROWS = 16
LANE = 128
DT_PAD = LANE
QK_SCALE = HEAD_DIM ** -0.5
VMEM_LIMIT = 56 * 1024 * 1024

BF16 = jnp.bfloat16
F32 = jnp.float32


def _cparams(sem):
    return pltpu.CompilerParams(dimension_semantics=sem, vmem_limit_bytes=VMEM_LIMIT)


def _const_spec(shape):
    nd = len(shape)
    return pl.BlockSpec(shape, lambda *_: (0,) * nd)


def _layer_spec(shape, layer):
    nd = len(shape)
    return pl.BlockSpec((None,) + tuple(shape), lambda *_: (layer,) + (0,) * nd)


def _dot(a, b):
    return jnp.dot(a, b, preferred_element_type=F32)


def _dot_nt(a, b):
    return lax.dot_general(a, b, (((1,), (1,)), ((), ())), preferred_element_type=F32)


def _silu(t):
    return t * (1.0 / (1.0 + jnp.exp(-t)))


def _rms(x, g):
    return (x * lax.rsqrt(jnp.mean(x * x, axis=-1, keepdims=True) + EPS)) * g


def _ada(x, g, shift, scale):
    return _rms(x, g) * (1.0 + scale) + shift


def _mod_kernel(c_ref, w_ref, b_ref, o_ref):
    act = _silu(c_ref[...]).astype(BF16)
    o_ref[...] = _dot(act, w_ref[...].astype(BF16)) + b_ref[...]


def _modulation(c_rows, w_mod, b_mod):
    return pl.pallas_call(
        _mod_kernel,
        out_shape=jax.ShapeDtypeStruct((DEPTH, N_MOD, MOD_ROWS, D_MODEL), F32),
        grid=(DEPTH, N_MOD),
        in_specs=[
            pl.BlockSpec((MOD_ROWS, D_MODEL), lambda l, i: (0, 0)),
            pl.BlockSpec((None, D_MODEL, D_MODEL), lambda l, i: (l, 0, i)),
            pl.BlockSpec((None, 1, D_MODEL), lambda l, i: (l, 0, i)),
        ],
        out_specs=pl.BlockSpec((None, None, MOD_ROWS, D_MODEL), lambda l, i: (l, i, 0, 0)),
        compiler_params=_cparams(("arbitrary", "arbitrary")),
        name="modulation",
    )(c_rows, w_mod, b_mod.reshape(DEPTH, 1, N_MOD * D_MODEL))


def _mod_spec(kind):
    return pl.BlockSpec((None, None, 1, D_MODEL), lambda b, t: (kind, b, 0, 0))


FFN_TM = 1024
FFN_CHUNKS = ((0, 1024), (1024, 2048), (2048, 2816))


def _ffn_kernel(*refs, final, nlat_split):
    refs = list(refs)
    x_ref = refs.pop(0)
    xc_ref = refs.pop(0) if nlat_split is not None else None
    sh_ref, sc_ref, gt_ref, g_ref, wg_ref, wu_ref, wd_ref = refs[:7]
    fg_ref = refs[7] if final else None
    o_ref = refs[-1]
    x = x_ref[...]
    if nlat_split is not None:
        x = jnp.where(pl.program_id(0) == nlat_split, xc_ref[...], x)
    xn = _ada(x, g_ref[...], sh_ref[...], sc_ref[...]).astype(BF16)
    acc = None
    for c0, c1 in FFN_CHUNKS:
        gate = _dot(xn, wg_ref[:, c0:c1])
        up = _dot(xn, wu_ref[:, c0:c1])
        h = (_silu(gate) * up).astype(BF16)
        part = _dot(h, wd_ref[c0:c1, :])
        acc = part if acc is None else acc + part
    y = x + (0.5 * gt_ref[...]) * acc
    if final:
        y = _rms(y, fg_ref[...])
    o_ref[...] = y


def _ffn(x_all, mod_l, kinds, gain, wg, wu, wd, layer, nb, final_gain=None, x_ctx=None):
    seq = x_all.shape[1]
    final = final_gain is not None
    tok = pl.BlockSpec((None, FFN_TM, D_MODEL), lambda b, t: (b, t, 0))
    if x_ctx is None:
        nlat_split = None
        in_specs, args = [tok], [x_all]
    else:
        nlat_split = x_all.shape[0]
        in_specs = [pl.BlockSpec((None, FFN_TM, D_MODEL), lambda b, t: (jnp.minimum(b, nlat_split - 1), t, 0)),
                    pl.BlockSpec((None, FFN_TM, D_MODEL), lambda b, t: (0, t, 0))]
        args = [x_all, x_ctx]
    in_specs += [
        _mod_spec(kinds[0]), _mod_spec(kinds[1]), _mod_spec(kinds[2]),
        _const_spec((1, D_MODEL)),
        _layer_spec((D_MODEL, D_FF), layer), _layer_spec((D_MODEL, D_FF), layer),
        _layer_spec((D_FF, D_MODEL), layer),
    ]
    args += [mod_l, mod_l, mod_l, gain.reshape(1, D_MODEL), wg, wu, wd]
    if final:
        in_specs.append(_const_spec((1, D_MODEL)))
        args.append(final_gain.reshape(1, D_MODEL))
    return pl.pallas_call(
        functools.partial(_ffn_kernel, final=final, nlat_split=nlat_split),
        out_shape=jax.ShapeDtypeStruct((nb, seq, D_MODEL), F32),
        grid=(nb, seq // FFN_TM),
        in_specs=in_specs,
        out_specs=pl.BlockSpec((None, FFN_TM, D_MODEL), lambda b, t: (b, t, 0)),
        compiler_params=_cparams(("arbitrary", "arbitrary")),
        name="ffn",
    )(*args)


PA, PB, PC, PZ, PX, PDT = 0, 512, 1280, 1792, 2048, 2816
PROJ_N = PDT + DT_PAD
PROJ_TM = 512
Q_HEAD_ORDER = (0, 2, 1, 3)


def _lane_iota(shape):
    return lax.broadcasted_iota(jnp.int32, shape, len(shape) - 1)


def _rope(t, cos, sin):
    w = t.shape[-1]
    first = (_lane_iota((1, w)) % 32) < 16
    rot = jnp.where(first, pltpu.roll(t, w - 16, 1), pltpu.roll(t, 16, 1))
    return t * cos[:, :w] + rot * sin[:, :w]


def _head_norm(t, gain):
    w = t.shape[-1]
    lane = _lane_iota((1, w))
    sq = t * t
    scale = jnp.zeros_like(t)
    for h in range(w // HEAD_DIM):
        m = (lane >= h * HEAD_DIM) & (lane < (h + 1) * HEAD_DIM)
        ms = jnp.sum(jnp.where(m, sq, 0.0), axis=-1, keepdims=True) * (1.0 / HEAD_DIM)
        scale = jnp.where(m, lax.rsqrt(ms + EPS), scale)
    return (t * scale) * gain[:, :w]


def _pair_heads(t):
    lo = _lane_iota((1, LANE)) < HEAD_DIM
    t0, t1 = t[:, 0:LANE], t[:, LANE:2 * LANE]
    return jnp.concatenate([jnp.where(lo, t0, pltpu.roll(t1, HEAD_DIM, 1)),
                            jnp.where(lo, pltpu.roll(t0, HEAD_DIM, 1), t1)], axis=1)


def _proj_kernel(x_ref, sh_ref, sc_ref, g_ref, w_ref, rope_ref, nq_ref, nk_ref,
                 qa_ref, qb_ref, qc_ref, z_ref, xbc_ref, dt_ref):
    xn = _ada(x_ref[...], g_ref[...], sh_ref[...], sc_ref[...]).astype(BF16)
    cos = rope_ref[:, 0:256]
    sin = rope_ref[:, 256:512]
    a = _dot(xn, w_ref[:, PA:PA + 512])
    qa_ref[:, 0:256] = _pair_heads(_rope(a[:, 0:256], cos, sin) * QK_SCALE).astype(BF16)
    qa_ref[:, 256:384] = _rope(a[:, 256:384], cos, sin).astype(BF16)
    qa_ref[:, 384:512] = a[:, 384:512].astype(BF16)
    b = _dot(xn, w_ref[:, PB:PB + 768])
    qb_ref[:, 0:256] = (b[:, 0:256] * QK_SCALE).astype(BF16)
    qb_ref[:, 256:768] = b[:, 256:768].astype(BF16)
    c = _dot(xn, w_ref[:, PC:PC + 512])
    qc_ref[:, 0:256] = _pair_heads(_rope(_head_norm(c[:, 0:256], nq_ref[...]), cos, sin) * QK_SCALE).astype(BF16)
    qc_ref[:, 256:384] = _rope(_head_norm(c[:, 256:384], nk_ref[...]), cos, sin).astype(BF16)
    qc_ref[:, 384:512] = c[:, 384:512].astype(BF16)
    z_ref[...] = _dot(xn, w_ref[:, PZ:PZ + 256])
    xbc_ref[...] = _dot(xn, w_ref[:, PX:PX + 768])
    dt_ref[...] = _dot(xn, w_ref[:, PDT:PDT + DT_PAD])


def _proj(x_all, mod_l, gain, w_proj, layer, rope_tab, nq, nk, nlat):
    nb, seq, _ = x_all.shape
    tok = lambda w: pl.BlockSpec((None, PROJ_TM, w), lambda b, t: (b, t, 0))
    outs = [(512, BF16), (768, BF16), (512, BF16), (256, F32), (768, F32), (DT_PAD, F32)]
    return pl.pallas_call(
        _proj_kernel,
        out_shape=[jax.ShapeDtypeStruct((nb, seq, w), dt) for w, dt in outs],
        grid=(nb, seq // PROJ_TM),
        in_specs=[
            tok(D_MODEL), _mod_spec(3), _mod_spec(4),
            _const_spec((1, D_MODEL)),
            _layer_spec((D_MODEL, PROJ_N), layer),
            pl.BlockSpec((None, PROJ_TM, 512), lambda b, t: (b // nlat, t, 0)),
            _const_spec((1, 256)), _const_spec((1, 256)),
        ],
        out_specs=[tok(w) for w, _ in outs],
        compiler_params=_cparams(("arbitrary", "arbitrary")),
        name="in_proj",
    )(x_all, mod_l, mod_l, gain.reshape(1, D_MODEL), w_proj, rope_tab, nq, nk)


def _gqa_core(q, keys, values, masks, sink_ref, t):
    lo = _lane_iota((1, LANE)) < HEAD_DIM
    t0, t1 = q[:, 0:LANE], q[:, LANE:2 * LANE]
    zero = jnp.zeros_like(t0)
    chain_scores = []
    for hk in range(2):
        if hk == 0:
            qs = jnp.concatenate([jnp.where(lo, t0, zero), jnp.where(lo, t1, zero)], axis=0)
        else:
            qs = jnp.concatenate([jnp.where(lo, zero, t0), jnp.where(lo, zero, t1)], axis=0)
        scores = []
        for k, m in zip(keys, masks):
            s = _dot_nt(qs, k)
            scores.append(s if m is None else jnp.where(m, s, NEG_INF))
        chain_scores.append(scores)
    outs = []
    for hk in range(2):
        extra = None
        if sink_ref is not None:
            row = lax.broadcasted_iota(jnp.int32, (2 * t, 1), 0)
            extra = jnp.where(row < t, sink_ref[2 * hk], sink_ref[2 * hk + 1])
        outs.append(_softmax_pv(chain_scores[hk], values, extra, spare_lanes=~lo if hk == 0 else lo))
    o0, o1 = outs
    return jnp.concatenate([jnp.where(lo, o0[0:t], o1[0:t]), jnp.where(lo, o0[t:2 * t], o1[t:2 * t])], axis=1)


def _softmax_pv(scores, values, extra_logit=None, spare_lanes=None):
    m = None
    for s in scores:
        sm = jnp.max(s, axis=-1, keepdims=True)
        m = sm if m is None else jnp.maximum(m, sm)
    if extra_logit is not None:
        m = jnp.maximum(m, extra_logit)
    acc = None
    for s, v in zip(scores, values):
        p = jnp.exp((s - m).astype(BF16))
        if spare_lanes is None:
            pv = _dot(p, jnp.concatenate([v, jnp.ones_like(v)], axis=1))
        else:
            pv = _dot(p, jnp.where(spare_lanes, jnp.ones_like(v), v))
        acc = pv if acc is None else acc + pv
    if spare_lanes is None:
        den = acc[:, LANE:2 * LANE]
        acc = acc[:, 0:LANE]
    else:
        den = pltpu.roll(acc, HEAD_DIM, 1)
    if extra_logit is not None:
        den = den + jnp.exp(extra_logit - m)
    return acc * (1.0 / den)


WINDOW_TQ = 256
DENSE_TQ = 128


def _gqa_kernel(*refs, window, seq, tq):
    if window:
        sink_ref, q_ref, k_ref, v_ref, kc_ref, vc_ref, o_ref = refs
    else:
        q_ref, k_ref, v_ref, kc_ref, vc_ref, o_ref = refs
    if window:
        nkw = tq + 2 * WA_WINDOW
        i = pl.program_id(1)
        start = pl.multiple_of(jnp.clip(i * tq - WA_WINDOW, 0, seq - nkw), LANE)
        kw = k_ref[pl.ds(start, nkw), :]
        vw = v_ref[pl.ds(start, nkw), :]
        qpos = i * tq + lax.broadcasted_iota(jnp.int32, (2 * tq, nkw), 0) % tq
        kpos = start + lax.broadcasted_iota(jnp.int32, (2 * tq, nkw), 1)
        mask = jnp.abs(qpos - kpos) <= WA_WINDOW
        sink = sink_ref
    else:
        kw, vw, mask, sink = k_ref[...], v_ref[...], None, None
    o = _gqa_core(q_ref[...], [kw, kc_ref[...]], [vw, vc_ref[...]], [mask, None], sink, tq)
    o_ref[...] = o.astype(BF16)


def _gqa_attention(qkv, sink, nlat, window):
    nb, seq, _ = qkv.shape
    tq = WINDOW_TQ if window else DENSE_TQ
    in_specs = [
        pl.BlockSpec((None, tq, 256), lambda b, i: (b, i, 0)),
        pl.BlockSpec((None, seq, LANE), lambda b, i: (b, 0, 2)),
        pl.BlockSpec((None, seq, LANE), lambda b, i: (b, 0, 3)),
        pl.BlockSpec((None, CTX_LEN, LANE), lambda b, i: (nlat, b, 2)),
        pl.BlockSpec((None, CTX_LEN, LANE), lambda b, i: (nlat, b, 3)),
    ]
    args = [qkv, qkv, qkv, qkv, qkv]
    if window:
        in_specs.insert(0, pl.BlockSpec(memory_space=pltpu.SMEM))
        args.insert(0, sink)
    return pl.pallas_call(
        functools.partial(_gqa_kernel, window=window, seq=seq, tq=tq),
        out_shape=jax.ShapeDtypeStruct((nlat, seq, BRANCH_W), BF16),
        grid=(nlat, seq // tq),
        in_specs=in_specs,
        out_specs=pl.BlockSpec((None, tq, BRANCH_W), lambda b, i: (b, i, 0)),
        compiler_params=_cparams(("arbitrary", "arbitrary")),
        name="window_attention" if window else "dense_attention",
    )(*args)


NA_STEP_ROWS = 4
NA_WIN_ROWS = NA_STEP_ROWS + NA_ROWS
NA_TQ = NA_STEP_ROWS * GRID_W
NA_TK = NA_WIN_ROWS * GRID_W


def _na_window_start(group, rows):
    return jnp.clip(group * NA_STEP_ROWS - NA_ROWS // 2, 0, rows - NA_WIN_ROWS)


def _na_kernel(q_ref, k_ref, v_ref, kc_ref, vc_ref, bias_ref, o_ref, *, rows):
    lo = _lane_iota((1, LANE)) < HEAD_DIM
    start = pl.multiple_of(_na_window_start(pl.program_id(0), rows) * GRID_W, GRID_W)
    scores = []
    for blk in range(2):
        cols = slice(blk * LANE, (blk + 1) * LANE)
        qb = q_ref[:, cols]
        zero = jnp.zeros_like(qb)
        qs = jnp.concatenate([jnp.where(lo, qb, zero), jnp.where(lo, zero, qb)], axis=0)
        s_nb = _dot_nt(qs, k_ref[pl.ds(start, NA_TK), cols]) + bias_ref[blk]
        scores.append([s_nb, _dot_nt(qs, kc_ref[:, cols])])
    outs = []
    for blk in range(2):
        cols = slice(blk * LANE, (blk + 1) * LANE)
        o = _softmax_pv(scores[blk], [v_ref[pl.ds(start, NA_TK), cols], vc_ref[:, cols]])
        outs.append(jnp.where(lo, o[0:NA_TQ], o[NA_TQ:2 * NA_TQ]))
    o_ref[...] = jnp.concatenate(outs, axis=1).astype(BF16)


def _na_attention(qkv, bias, layer, nlat):
    nb, seq, _ = qkv.shape
    rows = seq // GRID_W
    return pl.pallas_call(
        functools.partial(_na_kernel, rows=rows),
        out_shape=jax.ShapeDtypeStruct((nlat, seq, BRANCH_W), BF16),
        grid=(rows // NA_STEP_ROWS, nlat),
        in_specs=[
            pl.BlockSpec((None, NA_TQ, 256), lambda g, b: (b, g, 0)),
            pl.BlockSpec((None, seq, 256), lambda g, b: (b, 0, 1)),
            pl.BlockSpec((None, seq, 256), lambda g, b: (b, 0, 2)),
            pl.BlockSpec((None, CTX_LEN, 256), lambda g, b: (nlat, b, 1)),
            pl.BlockSpec((None, CTX_LEN, 256), lambda g, b: (nlat, b, 2)),
            pl.BlockSpec((None, None, 2, 2 * NA_TQ, NA_TK), lambda g, b: (layer, g, 0, 0, 0)),
        ],
        out_specs=pl.BlockSpec((None, NA_TQ, BRANCH_W), lambda g, b: (b, g, 0)),
        compiler_params=_cparams(("arbitrary", "arbitrary")),
        name="neighborhood_attention",
    )(qkv, qkv, qkv, qkv, qkv, bias)


def _ctx_kernel(sink_ref, qa_ref, ka_ref, va_ref, qb_ref, kb_ref, vb_ref, qc_ref, kc_ref, vc_ref,
                oa_ref, ob_ref, oc_ref):
    t = CTX_LEN
    oa_ref[...] = _gqa_core(qa_ref[...], [ka_ref[...]], [va_ref[...]], [None], sink_ref, t).astype(BF16)
    oc_ref[...] = _gqa_core(qc_ref[...], [kc_ref[...]], [vc_ref[...]], [None], None, t).astype(BF16)
    lo = _lane_iota((1, LANE)) < HEAD_DIM
    outs = []
    for blk in range(2):
        cols = slice(blk * LANE, (blk + 1) * LANE)
        qb = qb_ref[:, cols]
        zero = jnp.zeros_like(qb)
        qs = jnp.concatenate([jnp.where(lo, qb, zero), jnp.where(lo, zero, qb)], axis=0)
        o = _softmax_pv([_dot_nt(qs, kb_ref[:, cols])], [vb_ref[:, cols]])
        outs.append(jnp.where(lo, o[0:t], o[t:2 * t]))
    ob_ref[...] = jnp.concatenate(outs, axis=1).astype(BF16)


def _ctx_attention(sink, qa, qb, qc, nlat):
    blk = lambda w, j: pl.BlockSpec((None, CTX_LEN, w), lambda b: (nlat, b, j))
    out_spec = pl.BlockSpec((None, CTX_LEN, BRANCH_W), lambda b: (0, b, 0))
    return pl.pallas_call(
        _ctx_kernel,
        out_shape=[jax.ShapeDtypeStruct((1, nlat * CTX_LEN, BRANCH_W), BF16)] * 3,
        grid=(nlat,),
        in_specs=[
            pl.BlockSpec(memory_space=pltpu.SMEM),
            blk(256, 0), blk(LANE, 2), blk(LANE, 3),
            blk(256, 0), blk(256, 1), blk(256, 2),
            blk(256, 0), blk(LANE, 2), blk(LANE, 3),
        ],
        out_specs=[out_spec] * 3,
        compiler_params=_cparams(("arbitrary",)),
        name="context_attention",
    )(sink, qa, qa, qa, qb, qb, qb, qc, qc, qc)


CONV_PAD = 8
FIN_ROWS = 256


def _split3(t):
    hi = t.astype(BF16)
    r1 = t - hi.astype(F32)
    mid = r1.astype(BF16)
    return hi, mid, (r1 - mid.astype(F32)).astype(BF16)


HEAD_ROWS = 16
CONV_PHASES = 4
N_SLAB = SSM_CONV_CH // LANE


def _ssd_kernel(xbc_ref, xbcc_ref, dt_ref, dtc_ref, z_ref, zc_ref, cw_ref, cb_ref, dtb_ref, alog_ref,
                dvec_ref, nrm_ref, y_ref, yc_ref, pad_ref, u_ref, ct_ref, dt8_ref, w_ref, e_ref,
                yacc_ref, loc_ref, ce_ref, *, seq):
    ltot = CTX_LEN + seq
    nchunk = ltot // SSM_CHUNK
    nctx = CTX_LEN // SSM_CHUNK
    half = SSM_CONV // 2
    cs = SSM_CHUNK
    hr = HEAD_ROWS

    def conv_into(src_ref, n, dst):
        for sl in range(N_SLAB):
            pad_ref[sl, 0:CONV_PAD, :] = jnp.zeros((CONV_PAD, LANE), F32)
            pad_ref[sl, CONV_PAD + n:2 * CONV_PAD + n, :] = jnp.zeros((CONV_PAD, LANE), F32)
            pad_ref[sl, CONV_PAD:CONV_PAD + n, :] = src_ref[:, sl * LANE:(sl + 1) * LANE]
        q = FIN_ROWS // CONV_PHASES

        def block(i, carry):
            r0 = pl.multiple_of(i * FIN_ROWS, FIN_ROWS)
            for sl in range(N_SLAB):
                lanes = slice(sl * LANE, (sl + 1) * LANE)
                for p in range(CONV_PHASES):
                    acc = jnp.broadcast_to(cb_ref[:, lanes], (q, LANE))
                    for k in range(SSM_CONV):
                        tap = pad_ref[sl, pl.ds(r0 + CONV_PAD + p + k - half, q, stride=CONV_PHASES), :]
                        acc = acc + tap * cw_ref[k:k + 1, lanes]
                    u_ref[sl, pl.ds(dst + r0 + p, q, stride=CONV_PHASES), :] = _silu(acc)
            return carry

        lax.fori_loop(0, n // FIN_ROWS, block, 0)

    conv_into(xbcc_ref, CTX_LEN, 0)
    conv_into(xbc_ref, seq, CTX_LEN)

    row = lax.broadcasted_iota(jnp.int32, (cs, cs), 0)
    col = lax.broadcasted_iota(jnp.int32, (cs, cs), 1)
    causal = (col <= row, col >= row)
    lo = _lane_iota((1, LANE)) < SSM_STATE // 2

    def softplus(t):
        return jnp.maximum(t, 0.0) + jnp.log1p(jnp.exp(-jnp.abs(t)))

    sel = jnp.where(lax.broadcasted_iota(jnp.int32, (hr, LANE), 0) == lax.broadcasted_iota(jnp.int32, (hr, LANE), 1),
                    1.0, 0.0).astype(BF16)

    def heads_to_rows(src_ref):
        return sum(_dot_nt(sel, part) for part in _split3(src_ref[...]))

    dt_t = softplus(jnp.concatenate([heads_to_rows(dtc_ref), heads_to_rows(dt_ref)], axis=1) + dtb_ref[...])
    da_t = dt_t * (-jnp.exp(alog_ref[...]))

    def by_chunk(t):
        return jnp.concatenate([t[:, c * cs:(c + 1) * cs] for c in range(nchunk)], axis=0)

    dt_c, da_c = by_chunk(dt_t), by_chunk(da_t)
    tri = jnp.where(row <= col, 1.0, 0.0).astype(BF16)
    pre = sum(_dot(part, tri) for part in _split3(da_c))
    is_fwd = lax.broadcasted_iota(jnp.int32, (nchunk * hr, 1), 0) % hr < SSM_HEADS
    ct = jnp.where(is_fwd, pre, (pre[:, cs - 1:cs] - pre) + da_c)
    tot = jnp.where(is_fwd, ct[:, cs - 1:cs], ct[:, 0:1])
    ct_ref[...] = ct
    dt8_ref[...] = dt_c
    w_ref[...] = dt_c * jnp.exp(tot - ct)
    e_ref[...] = jnp.broadcast_to(jnp.exp(tot), (nchunk * hr, cs))

    def chunk_rows(c):
        return pl.ds(pl.multiple_of(c * cs, cs), cs)

    def head_rows(c):
        return pl.ds(pl.multiple_of(c * hr, hr), hr)

    def phase1(c, carry):
        rows = chunk_rows(c)
        ct8 = ct_ref[head_rows(c), :]
        dt8 = dt8_ref[head_rows(c), :]
        w8 = w_ref[head_rows(c), :]
        for g in range(SSM_GROUPS):
            gl = slice(g * LANE, (g + 1) * LANE)
            xg = u_ref[g, rows, :]
            bg = u_ref[SSM_GROUPS + g, rows, :]
            cg = u_ref[2 * SSM_GROUPS + g, rows, :]
            xcat = jnp.concatenate([jnp.where(lo, xg, 0.0), jnp.where(lo, 0.0, xg)], axis=0).astype(BF16)
            bt = bg.T
            cb = _dot_nt(cg.astype(BF16), bg.astype(BF16))
            mh = [None, None]
            ces, bts = [], []
            for d in range(2):
                for hh in range(2):
                    j = d * SSM_HEADS + 2 * g + hh
                    cum_j = jnp.broadcast_to(ct8[j:j + 1, :], (cs, cs)).T
                    seg = jnp.where(causal[d], cum_j - ct8[j:j + 1, :], NEG_INF)
                    term = jnp.exp(seg) * dt8[j:j + 1, :]
                    mh[hh] = term if d == 0 else mh[hh] + term
                    ces.append((cg * jnp.exp(cum_j)).astype(BF16))
                    bts.append((bt * w8[j:j + 1, :]).astype(BF16))
            m = jnp.concatenate([(cb * mh[0]).astype(BF16), (cb * mh[1]).astype(BF16)], axis=1)
            yacc_ref[rows, gl] = _dot(m, xcat)
            ce_ref[rows, g * 4 * LANE:(g + 1) * 4 * LANE] = jnp.concatenate(ces, axis=1)
            for d in range(2):
                st = _dot(jnp.concatenate([bts[2 * d], bts[2 * d + 1]], axis=1), xcat)
                loc_ref[c, :, d * SSM_INNER + g * LANE:d * SSM_INNER + (g + 1) * LANE] = st
        return carry

    lax.fori_loop(0, nchunk, phase1, 0, unroll=2)

    def scan(order, d):
        cols = slice(d * SSM_INNER, (d + 1) * SSM_INNER)

        def body(t, s):
            c = order(t)
            local = loc_ref[c, :, cols]
            loc_ref[c, :, cols] = s
            e8 = e_ref[head_rows(c), :]
            j0 = d * SSM_HEADS
            decay = jnp.concatenate([jnp.where(lo, e8[j0 + 2 * g:j0 + 2 * g + 1, :], e8[j0 + 2 * g + 1:j0 + 2 * g + 2, :])
                                     for g in range(SSM_GROUPS)], axis=1)
            return s * decay + local
        lax.fori_loop(0, nchunk, body, jnp.zeros((SSM_STATE, SSM_INNER), F32))

    scan(lambda t: t, 0)
    scan(lambda t: jnp.where(t < nctx, nctx - 1 - t, nchunk + nctx - 1 - t), 1)

    def finish_chunk(c, z_rows, out_ref, out_rows):
        rows = chunk_rows(c)
        s_in = loc_ref[c]
        ys = []
        for g in range(SSM_GROUPS):
            gl = slice(g * LANE, (g + 1) * LANE)
            sf = s_in[:, gl]
            sb = s_in[:, SSM_INNER + g * LANE:SSM_INNER + (g + 1) * LANE]
            scat = jnp.concatenate([jnp.where(lo, sf, 0.0), jnp.where(lo, 0.0, sf),
                                    jnp.where(lo, sb, 0.0), jnp.where(lo, 0.0, sb)], axis=0).astype(BF16)
            ys.append(yacc_ref[rows, gl] + _dot(ce_ref[rows, g * 4 * LANE:(g + 1) * 4 * LANE], scat))
        xs = jnp.concatenate([u_ref[g, rows, :] for g in range(SSM_GROUPS)], axis=1)
        y = jnp.concatenate(ys, axis=1) + dvec_ref[...] * xs
        y = y * _silu(z_rows)
        out_ref[out_rows, :] = _rms(y, nrm_ref[...]).astype(BF16)

    for c in range(nctx):
        finish_chunk(c, zc_ref[c * cs:(c + 1) * cs, :], yc_ref, slice(c * cs, (c + 1) * cs))

    def finish_body(cl, carry):
        out_rows = chunk_rows(cl)
        finish_chunk(cl + nctx, z_ref[out_rows, :], y_ref, out_rows)
        return carry

    lax.fori_loop(0, nchunk - nctx, finish_body, 0, unroll=2)


def _ssd(xbc, dt, z, conv_w, conv_b, dt_bias, a_log, dvec, nrm, nlat):
    nb, seq, _ = xbc.shape
    ltot = CTX_LEN + seq
    nchunk = ltot // SSM_CHUNK
    lat = lambda w: pl.BlockSpec((None, seq, w), lambda b: (b, 0, 0))
    ctx = lambda w: pl.BlockSpec((None, CTX_LEN, w), lambda b: (nlat, b, 0))
    return pl.pallas_call(
        functools.partial(_ssd_kernel, seq=seq),
        out_shape=[jax.ShapeDtypeStruct((nlat, seq, SSM_INNER), BF16),
                   jax.ShapeDtypeStruct((nlat, CTX_LEN, SSM_INNER), BF16)],
        grid=(nlat,),
        in_specs=[
            lat(SSM_CONV_CH), ctx(SSM_CONV_CH), lat(DT_PAD), ctx(DT_PAD), lat(SSM_INNER), ctx(SSM_INNER),
            _const_spec((8, SSM_CONV_CH)), _const_spec((1, SSM_CONV_CH)),
            _const_spec((HEAD_ROWS, 1)), _const_spec((HEAD_ROWS, 1)),
            _const_spec((1, SSM_INNER)), _const_spec((1, SSM_INNER)),
        ],
        out_specs=[pl.BlockSpec((None, seq, SSM_INNER), lambda b: (b, 0, 0)),
                   pl.BlockSpec((None, CTX_LEN, SSM_INNER), lambda b: (b, 0, 0))],
        scratch_shapes=[
            pltpu.VMEM((N_SLAB, seq + 2 * CONV_PAD, LANE), F32),
            pltpu.VMEM((N_SLAB, ltot, LANE), F32),
            pltpu.VMEM((nchunk * HEAD_ROWS, SSM_CHUNK), F32),
            pltpu.VMEM((nchunk * HEAD_ROWS, SSM_CHUNK), F32),
            pltpu.VMEM((nchunk * HEAD_ROWS, SSM_CHUNK), F32),
            pltpu.VMEM((nchunk * HEAD_ROWS, SSM_CHUNK), F32),
            pltpu.VMEM((ltot, SSM_INNER), F32),
            pltpu.VMEM((nchunk, SSM_STATE, 2 * SSM_INNER), F32),
            pltpu.VMEM((ltot, 2 * SSM_GROUPS * 2 * SSM_STATE), BF16),
        ],
        compiler_params=_cparams(("arbitrary",)),
        name="ssd",
    )(xbc, xbc, dt, dt, z, z, conv_w, conv_b, dt_bias, a_log, dvec, nrm)


MERGE_TM = 512


def _merge_kernel(*refs, nlat_split):
    x_ref, sh_ref, sc_ref, gt_ref, g_ref = refs[:5]
    y_refs = refs[5:9]
    yc_refs = refs[9:13] if nlat_split is not None else None
    wgl_ref, bg_ref, wbr_ref, wo_ref, o_ref = refs[-5:]
    x = x_ref[...]
    xn = _ada(x, g_ref[...], sh_ref[...], sc_ref[...]).astype(BF16)
    mix = None
    for k in range(N_BRANCH):
        cols = slice(k * D_MODEL, (k + 1) * D_MODEL)
        pre = _dot(xn, wgl_ref[:, cols]) + bg_ref[:, cols]
        gate = 1.0 / (1.0 + jnp.exp(-pre))
        y = y_refs[k][...]
        if nlat_split is not None:
            y = jnp.where(pl.program_id(0) == nlat_split, yc_refs[k][...], y)
        term = gate * _dot(y, wbr_ref[k])
        mix = term if mix is None else mix + term
    o_ref[...] = x + gt_ref[...] * _dot(mix.astype(BF16), wo_ref[...])


def _merge(x_all, mod_l, gain, ys, ys_ctx, w_gate, b_gate, w_branch, w_out, layer, nb):
    seq = x_all.shape[1]
    nlat = ys[0].shape[0]
    tok = lambda w: pl.BlockSpec((None, MERGE_TM, w), lambda b, t: (b, t, 0))
    lat = pl.BlockSpec((None, MERGE_TM, BRANCH_W), lambda b, t: (jnp.minimum(b, nlat - 1), t, 0))
    in_specs = [tok(D_MODEL), _mod_spec(3), _mod_spec(4), _mod_spec(5), _const_spec((1, D_MODEL))] + [lat] * 4
    args = [x_all, mod_l, mod_l, mod_l, gain.reshape(1, D_MODEL), *ys]
    if ys_ctx is not None:
        in_specs += [pl.BlockSpec((None, MERGE_TM, BRANCH_W), lambda b, t: (0, t, 0))] * 4
        args += list(ys_ctx)
    in_specs += [
        _layer_spec((D_MODEL, N_BRANCH * D_MODEL), layer), _layer_spec((1, N_BRANCH * D_MODEL), layer),
        _layer_spec((N_BRANCH, BRANCH_W, D_MODEL), layer), _layer_spec((D_MODEL, D_MODEL), layer),
    ]
    args += [w_gate, b_gate.reshape(DEPTH, 1, N_BRANCH * D_MODEL), w_branch, w_out]
    return pl.pallas_call(
        functools.partial(_merge_kernel, nlat_split=nlat if ys_ctx is not None else None),
        out_shape=jax.ShapeDtypeStruct((nb, seq, D_MODEL), F32),
        grid=(nb, seq // MERGE_TM),
        in_specs=in_specs,
        out_specs=tok(D_MODEL),
        compiler_params=_cparams(("arbitrary", "arbitrary")),
        name="merge",
    )(*args)


def _rope_table(seq):
    pos = jnp.arange(seq)
    axes = jnp.stack([pos // GRID_W, pos % GRID_W], axis=-1).astype(F32)
    quarter = HEAD_DIM // 4
    inv = 1.0 / (ROPE_THETA ** (jnp.arange(quarter, dtype=F32) * 4.0 / HEAD_DIM))
    ang = axes[:, :, None] * inv
    cos, sin = jnp.cos(ang), jnp.sin(ang)
    cos_h = jnp.concatenate([cos, cos], axis=-1).reshape(seq, HEAD_DIM)
    sin_h = jnp.concatenate([-sin, sin], axis=-1).reshape(seq, HEAD_DIM)
    lat = jnp.concatenate([jnp.tile(cos_h, (1, 4)), jnp.tile(sin_h, (1, 4))], axis=-1)
    ident = jnp.concatenate([jnp.ones((seq, 256), F32), jnp.zeros((seq, 256), F32)], axis=-1)
    return jnp.stack([lat, ident])


def _permute_heads(w, axis):
    parts = jnp.split(w, 4, axis=axis)
    return jnp.concatenate([parts[j] for j in Q_HEAD_ORDER], axis=axis)


CAST_SPLIT = 4


def _cast_kernel(w_ref, o_ref):
    o_ref[...] = w_ref[...].astype(BF16)


def _cast_bf16(w):
    depth, r, c = w.shape
    tr = r // CAST_SPLIT
    spec = pl.BlockSpec((None, tr, c), lambda l, i: (l, i, 0))
    return pl.pallas_call(
        _cast_kernel,
        out_shape=jax.ShapeDtypeStruct(w.shape, BF16),
        grid=(depth, CAST_SPLIT),
        in_specs=[spec], out_specs=spec,
        compiler_params=_cparams(("arbitrary", "arbitrary")),
        name="cast_bf16",
    )(w)


MXU_TILE = 256
GATE_COL0 = PDT + 2 * SSM_HEADS


def _transpose_cast_kernel(w_ref, o_ref):
    eye = jnp.where(lax.broadcasted_iota(jnp.int32, (MXU_TILE, MXU_TILE), 0)
                    == lax.broadcasted_iota(jnp.int32, (MXU_TILE, MXU_TILE), 1), 1.0, 0.0).astype(BF16)
    for k0 in range(0, D_MODEL, MXU_TILE):
        blk = w_ref[:, k0:k0 + MXU_TILE].astype(BF16)
        o_ref[k0:k0 + MXU_TILE, :] = _dot_nt(eye, blk).astype(BF16)


def _pack_w_in(w_in_t):
    depth = w_in_t.shape[0]
    proj = pl.pallas_call(
        _transpose_cast_kernel,
        out_shape=jax.ShapeDtypeStruct((depth, D_MODEL, PROJ_N), BF16),
        grid=(depth,),
        in_specs=[pl.BlockSpec((None, PROJ_N, D_MODEL), lambda l: (l, 0, 0))],
        out_specs=pl.BlockSpec((None, D_MODEL, PROJ_N), lambda l: (l, 0, 0)),
        compiler_params=_cparams(("arbitrary",)),
        name="pack_proj",
    )(w_in_t)
    gate = pl.pallas_call(
        lambda w_ref, o_ref: _transpose_cast_kernel(w_ref.at[0], o_ref),
        out_shape=jax.ShapeDtypeStruct((depth, D_MODEL, N_BRANCH * D_MODEL), BF16),
        grid=(depth, N_BRANCH),
        in_specs=[pl.BlockSpec((pl.Element(1), pl.Element(D_MODEL), pl.Element(D_MODEL)),
                               lambda l, k: (l, pl.multiple_of(GATE_COL0 + k * D_MODEL, 8), 0))],
        out_specs=pl.BlockSpec((None, D_MODEL, D_MODEL), lambda l, k: (l, 0, k)),
        compiler_params=_cparams(("arbitrary", "arbitrary")),
        name="pack_gate",
    )(w_in_t)
    return proj, gate


NA_BANDS = 2 * NA_ROWS - 1


def _na_bias_kernel(rpb_ref, o_ref, band_ref, *, rows):
    group = pl.program_id(1)
    c = lax.broadcasted_iota(jnp.int32, (GRID_W, LANE), 0)
    kc = lax.broadcasted_iota(jnp.int32, (GRID_W, LANE), 1)
    lo = kc < GRID_W

    @pl.when(group == 0)
    def _():
        qstart = jnp.clip(c - NA_COLS // 2, 0, GRID_W - NA_COLS)
        ok = (kc >= qstart) & (kc < qstart + NA_COLS)
        for h in range(2 * 2):
            for dr in range(NA_BANDS):
                line = jnp.broadcast_to(rpb_ref[h, dr:dr + 1, :], (GRID_W, LANE))
                skew = pltpu.roll(line, LANE - (NA_COLS - 1), 1, stride=1, stride_axis=0)
                band_ref[h, dr] = jnp.where(ok, skew, NEG_INF)

    w0 = _na_window_start(group, rows)
    for h in range(2 * 2):
        blk, g = divmod(h, 2)
        for qi in range(NA_STEP_ROWS):
            r = group * NA_STEP_ROWS + qi
            rs = jnp.clip(r - NA_ROWS // 2, 0, rows - NA_ROWS)
            for m in range(NA_WIN_ROWS // 2):
                halves = []
                for kr in (w0 + 2 * m, w0 + 2 * m + 1):
                    in_window = (kr >= rs) & (kr < rs + NA_ROWS)
                    dr = jnp.clip(kr - r + NA_ROWS - 1, 0, NA_BANDS - 1)
                    halves.append(jnp.where(in_window, band_ref[h, dr], NEG_INF))
                pair = jnp.where(lo, halves[0], pltpu.roll(halves[1], GRID_W, 1))
                o_ref[blk, g * NA_TQ + qi * GRID_W:g * NA_TQ + (qi + 1) * GRID_W, m * LANE:(m + 1) * LANE] = pair


def _na_bias_table(rpb, rows):
    depth, heads = rpb.shape[:2]
    groups = rows // NA_STEP_ROWS
    padded = jnp.pad(rpb, ((0, 0), (0, 0), (0, 16 - rpb.shape[2]), (0, LANE - rpb.shape[3])))
    return pl.pallas_call(
        functools.partial(_na_bias_kernel, rows=rows),
        out_shape=jax.ShapeDtypeStruct((depth, groups, 2, 2 * NA_TQ, NA_TK), F32),
        grid=(depth, groups),
        in_specs=[pl.BlockSpec((None, heads, 16, LANE), lambda l, g: (l, 0, 0, 0))],
        out_specs=pl.BlockSpec((None, None, 2, 2 * NA_TQ, NA_TK), lambda l, g: (l, g, 0, 0, 0)),
        scratch_shapes=[pltpu.VMEM((heads, 16, GRID_W, LANE), F32)],
        compiler_params=_cparams(("arbitrary", "arbitrary")),
        name="na_bias",
    )(padded)


def kernel(x, c, ctx, c_ctx, w_mod, b_mod, norm_ffn1, ffn1_w_gate, ffn1_w_up, ffn1_w_down, norm_mix, w_in, b_gate,
           attn_sink, na_rpb, qk_norm_q, qk_norm_k, ssm_conv_w, ssm_conv_b, ssm_dt_bias, ssm_a_log, ssm_d, ssm_norm,
           w_branch, w_out, norm_ffn2, ffn2_w_gate, ffn2_w_up, ffn2_w_down, final_norm):
    nlat, seq, _ = x.shape
    assert nlat * CTX_LEN == seq and nlat + 1 <= MOD_ROWS
    nb = nlat + 1
    c_rows = jnp.concatenate([c, c_ctx[None], jnp.zeros((MOD_ROWS - nb, D_MODEL), F32)], axis=0)
    mod = _modulation(c_rows, w_mod, b_mod).reshape(DEPTH, N_MOD, MOD_ROWS, 1, D_MODEL)
    rope_tab = _rope_table(seq)
    head_col = lambda v: jnp.pad(v.reshape(-1, 1), ((0, HEAD_ROWS - v.size), (0, 0)))

    ffn1_w = [_cast_bf16(w) for w in (ffn1_w_gate, ffn1_w_up, ffn1_w_down)]
    ffn2_w = [_cast_bf16(w) for w in (ffn2_w_gate, ffn2_w_up, ffn2_w_down)]
    w_proj, w_gl = _pack_w_in(jnp.swapaxes(w_in, 1, 2))
    w_o = _cast_bf16(w_out)
    wbr = jnp.stack([_permute_heads(w_branch[:, 0], 1), w_branch[:, 1],
                     _permute_heads(w_branch[:, 2], 1), w_branch[:, 3]], axis=1).astype(BF16)
    na_bias = _na_bias_table(na_rpb, seq // GRID_W)

    for l in range(DEPTH):
        with_ctx = l < DEPTH - 1
        last = l == DEPTH - 1
        nq = jnp.tile(qk_norm_q[l], 4).reshape(1, 256)
        nk = jnp.tile(qk_norm_k[l], 4).reshape(1, 256)
        conv_w = jnp.pad(ssm_conv_w[l], ((0, 8 - SSM_CONV), (0, 0)))
        dvec = jnp.repeat(ssm_d[l], SSM_INNER // SSM_HEADS).reshape(1, SSM_INNER)

        if l == 0:
            x_all = _ffn(x, mod[l], (0, 1, 2), norm_ffn1[l], *ffn1_w, l, nb, x_ctx=ctx.reshape(1, seq, D_MODEL))
        else:
            x_all = _ffn(x_all, mod[l], (0, 1, 2), norm_ffn1[l], *ffn1_w, l, nb)
        qa, qb, qc, z, xbc, dt = _proj(x_all, mod[l], norm_mix[l], w_proj, l, rope_tab, nq, nk, nlat)
        ya = _gqa_attention(qa, attn_sink[l], nlat, window=True)
        yb = _na_attention(qb, na_bias, l, nlat)
        yc = _gqa_attention(qc, None, nlat, window=False)
        yd, yd_ctx = _ssd(xbc, dt, z, conv_w, ssm_conv_b[l].reshape(1, -1), head_col(ssm_dt_bias[l]),
                          head_col(ssm_a_log[l]), dvec, ssm_norm[l].reshape(1, -1), nlat)
        nbm = nb if with_ctx else nlat
        ys_ctx = None
        if with_ctx:
            ys_ctx = (*_ctx_attention(attn_sink[l], qa, qb, qc, nlat), yd_ctx.reshape(1, seq, SSM_INNER))
        x_all = _merge(x_all, mod[l], norm_mix[l], (ya, yb, yc, yd), ys_ctx, w_gl, b_gate, wbr, w_o, l, nbm)
        x_all = _ffn(x_all, mod[l], (6, 7, 8), norm_ffn2[l], *ffn2_w, l, nbm,
                     final_gain=final_norm if last else None)
    return x_all
```

```python
import functools

import jax
import jax.numpy as jnp
from jax import lax
from jax.experimental import pallas as pl
from jax.experimental.pallas import tpu as pltpu

D_MODEL = 1024
DEPTH = 4
CTX_LEN = 256
GRID_W = 64
HEAD_DIM = 64
ROPE_THETA = 10000.0
EPS = 1e-6
NEG_INF = -1e30
D_FF = 2816
N_BRANCH = 4
BRANCH_W = 256
WA_WINDOW = 128
NA_ROWS = 8
NA_COLS = 16
SSM_HEADS = 4
SSM_GROUPS = 2
SSM_STATE = 128
SSM_CONV = 5
SSM_CHUNK = 128
SSM_INNER = 256
SSM_CONV_CH = 768
N_MOD = 9
MOD_ROWS = 16
LANE = 128
DT_PAD = LANE
QK_SCALE = HEAD_DIM ** -0.5
VMEM_LIMIT = 56 * 1024 * 1024

BF16 = jnp.bfloat16
F32 = jnp.float32


def _cparams(sem):
    return pltpu.CompilerParams(dimension_semantics=sem, vmem_limit_bytes=VMEM_LIMIT)


def _const_spec(shape):
    nd = len(shape)
    return pl.BlockSpec(shape, lambda *_: (0,) * nd)


def _layer_spec(shape, layer):
    nd = len(shape)
    return pl.BlockSpec((None,) + tuple(shape), lambda *_: (layer,) + (0,) * nd)


def _dot(a, b):
    return jnp.dot(a, b, preferred_element_type=F32)


def _dot_nt(a, b):
    return lax.dot_general(a, b, (((1,), (1,)), ((), ())), preferred_element_type=F32)


def _silu(t):
    return t * (1.0 / (1.0 + jnp.exp(-t)))


def _rms(x, g):
    return (x * lax.rsqrt(jnp.mean(x * x, axis=-1, keepdims=True) + EPS)) * g


def _ada(x, g, shift, scale):
    return _rms(x, g) * (1.0 + scale) + shift


def _mod_kernel(c_ref, w_ref, b_ref, o_ref):
    act = _silu(c_ref[...]).astype(BF16)
    o_ref[...] = _dot(act, w_ref[...].astype(BF16)) + b_ref[...]


def _modulation(c_rows, w_mod, b_mod):
    return pl.pallas_call(
        _mod_kernel,
        out_shape=jax.ShapeDtypeStruct((DEPTH, N_MOD, MOD_ROWS, D_MODEL), F32),
        grid=(DEPTH, N_MOD),
        in_specs=[
            pl.BlockSpec((MOD_ROWS, D_MODEL), lambda l, i: (0, 0)),
            pl.BlockSpec((None, D_MODEL, D_MODEL), lambda l, i: (l, 0, i)),
            pl.BlockSpec((None, 1, D_MODEL), lambda l, i: (l, 0, i)),
        ],
        out_specs=pl.BlockSpec((None, None, MOD_ROWS, D_MODEL), lambda l, i: (l, i, 0, 0)),
        compiler_params=_cparams(("arbitrary", "arbitrary")),
        name="modulation",
    )(c_rows, w_mod, b_mod.reshape(DEPTH, 1, N_MOD * D_MODEL))


def _mod_spec(kind):
    return pl.BlockSpec((None, None, 1, D_MODEL), lambda b, t: (kind, b, 0, 0))


FFN_TM = 1024
FFN_CHUNKS = ((0, 1024), (1024, 2048), (2048, 2816))


def _ffn_kernel(*refs, final, nlat_split):
    refs = list(refs)
    x_ref = refs.pop(0)
    xc_ref = refs.pop(0) if nlat_split is not None else None
    sh_ref, sc_ref, gt_ref, g_ref, wg_ref, wu_ref, wd_ref = refs[:7]
    fg_ref = refs[7] if final else None
    o_ref = refs[-1]
    x = x_ref[...]
    if nlat_split is not None:
        x = jnp.where(pl.program_id(0) == nlat_split, xc_ref[...], x)
    xn = _ada(x, g_ref[...], sh_ref[...], sc_ref[...]).astype(BF16)
    acc = None
    for c0, c1 in FFN_CHUNKS:
        gate = _dot(xn, wg_ref[:, c0:c1])
        up = _dot(xn, wu_ref[:, c0:c1])
        h = (_silu(gate) * up).astype(BF16)
        part = _dot(h, wd_ref[c0:c1, :])
        acc = part if acc is None else acc + part
    y = x + (0.5 * gt_ref[...]) * acc
    if final:
        y = _rms(y, fg_ref[...])
    o_ref[...] = y


def _ffn(x_all, mod_l, kinds, gain, wg, wu, wd, layer, nb, final_gain=None, x_ctx=None):
    seq = x_all.shape[1]
    final = final_gain is not None
    tok = pl.BlockSpec((None, FFN_TM, D_MODEL), lambda b, t: (b, t, 0))
    if x_ctx is None:
        nlat_split = None
        in_specs, args = [tok], [x_all]
    else:
        nlat_split = x_all.shape[0]
        in_specs = [pl.BlockSpec((None, FFN_TM, D_MODEL), lambda b, t: (jnp.minimum(b, nlat_split - 1), t, 0)),
                    pl.BlockSpec((None, FFN_TM, D_MODEL), lambda b, t: (0, t, 0))]
        args = [x_all, x_ctx]
    in_specs += [
        _mod_spec(kinds[0]), _mod_spec(kinds[1]), _mod_spec(kinds[2]),
        _const_spec((1, D_MODEL)),
        _layer_spec((D_MODEL, D_FF), layer), _layer_spec((D_MODEL, D_FF), layer),
        _layer_spec((D_FF, D_MODEL), layer),
    ]
    args += [mod_l, mod_l, mod_l, gain.reshape(1, D_MODEL), wg, wu, wd]
    if final:
        in_specs.append(_const_spec((1, D_MODEL)))
        args.append(final_gain.reshape(1, D_MODEL))
    return pl.pallas_call(
        functools.partial(_ffn_kernel, final=final, nlat_split=nlat_split),
        out_shape=jax.ShapeDtypeStruct((nb, seq, D_MODEL), F32),
        grid=(nb, seq // FFN_TM),
        in_specs=in_specs,
        out_specs=pl.BlockSpec((None, FFN_TM, D_MODEL), lambda b, t: (b, t, 0)),
        compiler_params=_cparams(("arbitrary", "arbitrary")),
        name="ffn",
    )(*args)


PA, PB, PC, PZ, PX, PDT = 0, 512, 1280, 1792, 2048, 2816
PROJ_N = PDT + DT_PAD
PROJ_TM = 512
Q_HEAD_ORDER = (0, 2, 1, 3)


def _lane_iota(shape):
    return lax.broadcasted_iota(jnp.int32, shape, len(shape) - 1)


def _rope(t, cos, sin):
    w = t.shape[-1]
    first = (_lane_iota((1, w)) % 32) < 16
    rot = jnp.where(first, pltpu.roll(t, w - 16, 1), pltpu.roll(t, 16, 1))
    return t * cos[:, :w] + rot * sin[:, :w]


def _head_norm(t, gain):
    w = t.shape[-1]
    lane = _lane_iota((1, w))
    sq = t * t
    scale = jnp.zeros_like(t)
    for h in range(w // HEAD_DIM):
        m = (lane >= h * HEAD_DIM) & (lane < (h + 1) * HEAD_DIM)
        ms = jnp.sum(jnp.where(m, sq, 0.0), axis=-1, keepdims=True) * (1.0 / HEAD_DIM)
        scale = jnp.where(m, lax.rsqrt(ms + EPS), scale)
    return (t * scale) * gain[:, :w]


def _pair_heads(t):
    lo = _lane_iota((1, LANE)) < HEAD_DIM
    t0, t1 = t[:, 0:LANE], t[:, LANE:2 * LANE]
    return jnp.concatenate([jnp.where(lo, t0, pltpu.roll(t1, HEAD_DIM, 1)),
                            jnp.where(lo, pltpu.roll(t0, HEAD_DIM, 1), t1)], axis=1)


def _proj_kernel(x_ref, sh_ref, sc_ref, g_ref, w_ref, rope_ref, nq_ref, nk_ref,
                 qa_ref, qb_ref, qc_ref, z_ref, xbc_ref, dt_ref):
    xn = _ada(x_ref[...], g_ref[...], sh_ref[...], sc_ref[...]).astype(BF16)
    cos = rope_ref[:, 0:256]
    sin = rope_ref[:, 256:512]
    a = _dot(xn, w_ref[:, PA:PA + 512])
    qa_ref[:, 0:256] = _pair_heads(_rope(a[:, 0:256], cos, sin) * QK_SCALE).astype(BF16)
    qa_ref[:, 256:384] = _rope(a[:, 256:384], cos, sin).astype(BF16)
    qa_ref[:, 384:512] = a[:, 384:512].astype(BF16)
    b = _dot(xn, w_ref[:, PB:PB + 768])
    qb_ref[:, 0:256] = (b[:, 0:256] * QK_SCALE).astype(BF16)
    qb_ref[:, 256:768] = b[:, 256:768].astype(BF16)
    c = _dot(xn, w_ref[:, PC:PC + 512])
    qc_ref[:, 0:256] = _pair_heads(_rope(_head_norm(c[:, 0:256], nq_ref[...]), cos, sin) * QK_SCALE).astype(BF16)
    qc_ref[:, 256:384] = _rope(_head_norm(c[:, 256:384], nk_ref[...]), cos, sin).astype(BF16)
    qc_ref[:, 384:512] = c[:, 384:512].astype(BF16)
    z_ref[...] = _dot(xn, w_ref[:, PZ:PZ + 256])
    xbc_ref[...] = _dot(xn, w_ref[:, PX:PX + 768])
    dt_ref[...] = _dot(xn, w_ref[:, PDT:PDT + DT_PAD])


def _proj(x_all, mod_l, gain, w_proj, layer, rope_tab, nq, nk, nlat):
    nb, seq, _ = x_all.shape
    tok = lambda w: pl.BlockSpec((None, PROJ_TM, w), lambda b, t: (b, t, 0))
    outs = [(512, BF16), (768, BF16), (512, BF16), (256, F32), (768, F32), (DT_PAD, F32)]
    return pl.pallas_call(
        _proj_kernel,
        out_shape=[jax.ShapeDtypeStruct((nb, seq, w), dt) for w, dt in outs],
        grid=(nb, seq // PROJ_TM),
        in_specs=[
            tok(D_MODEL), _mod_spec(3), _mod_spec(4),
            _const_spec((1, D_MODEL)),
            _layer_spec((D_MODEL, PROJ_N), layer),
            pl.BlockSpec((None, PROJ_TM, 512), lambda b, t: (b // nlat, t, 0)),
            _const_spec((1, 256)), _const_spec((1, 256)),
        ],
        out_specs=[tok(w) for w, _ in outs],
        compiler_params=_cparams(("arbitrary", "arbitrary")),
        name="in_proj",
    )(x_all, mod_l, mod_l, gain.reshape(1, D_MODEL), w_proj, rope_tab, nq, nk)


def _gqa_core(blocks, sink_ref):
    lo = _lane_iota((1, LANE)) < HEAD_DIM
    chain_scores = []
    for q, keys, _, biases in blocks:
        t0, t1 = q[:, 0:LANE], q[:, LANE:2 * LANE]
        zero = jnp.zeros_like(t0)
        for hk in range(2):
            if hk == 0:
                qs = jnp.concatenate([jnp.where(lo, t0, zero), jnp.where(lo, t1, zero)], axis=0)
            else:
                qs = jnp.concatenate([jnp.where(lo, zero, t0), jnp.where(lo, zero, t1)], axis=0)
            scores = []
            for k, bias in zip(keys, biases):
                s = _dot_nt(qs, k)
                scores.append(s if bias is None else s + bias)
            chain_scores.append(scores)
    results = []
    for sb, (q, _, values, _) in enumerate(blocks):
        t = q.shape[0]
        outs = []
        for hk in range(2):
            extra = None
            if sink_ref is not None:
                row = lax.broadcasted_iota(jnp.int32, (2 * t, 1), 0)
                extra = jnp.where(row < t, sink_ref[2 * hk], sink_ref[2 * hk + 1])
            outs.append(_softmax_pv(chain_scores[2 * sb + hk], values, extra, spare_lanes=~lo if hk == 0 else lo))
        o0, o1 = outs
        results.append(jnp.concatenate([jnp.where(lo, o0[0:t], o1[0:t]),
                                        jnp.where(lo, o0[t:2 * t], o1[t:2 * t])], axis=1))
    return results


def _softmax_pv(scores, values, extra_logit=None, spare_lanes=None):
    m = None
    for s in scores:
        sm = jnp.max(s, axis=-1, keepdims=True)
        m = sm if m is None else jnp.maximum(m, sm)
    if extra_logit is not None:
        m = jnp.maximum(m, extra_logit)
    acc = None
    for s, v in zip(scores, values):
        p = jnp.exp((s - m).astype(BF16))
        if spare_lanes is None:
            pv = _dot(p, jnp.concatenate([v, jnp.ones_like(v)], axis=1))
        else:
            pv = _dot(p, jnp.where(spare_lanes, jnp.ones_like(v), v))
        acc = pv if acc is None else acc + pv
    if spare_lanes is None:
        den = acc[:, LANE:2 * LANE]
        acc = acc[:, 0:LANE]
    else:
        den = pltpu.roll(acc, HEAD_DIM, 1)
    if extra_logit is not None:
        den = den + jnp.exp(extra_logit - m)
    return acc * (1.0 / den)


WINDOW_TQ = 256
WINDOW_SUB_BLOCKS = 4
DENSE_TQ = 128
DENSE_SUB_BLOCKS = 4


def _gqa_kernel(*refs, window, seq, tq, sub_blocks):
    if window:
        sink_ref, band_ref, q_ref, k_ref, v_ref, kc_ref, vc_ref, o_ref = refs
    else:
        q_ref, k_ref, v_ref, kc_ref, vc_ref, o_ref = refs
        sink_ref = None
    blocks = []
    for sb in range(sub_blocks):
        q = q_ref[sb * tq:(sb + 1) * tq, :]
        if window:
            nkw = tq + 2 * WA_WINDOW
            q0 = (pl.program_id(1) * sub_blocks + sb) * tq
            start = pl.multiple_of(jnp.clip(q0 - WA_WINDOW, 0, seq - nkw), LANE)
            bias = band_ref[(q0 - start) // WA_WINDOW]
            blocks.append((q, [k_ref[pl.ds(start, nkw), :], kc_ref[...]], [v_ref[pl.ds(start, nkw), :], vc_ref[...]],
                           [bias, None]))
        else:
            blocks.append((q, [k_ref[...], kc_ref[...]], [v_ref[...], vc_ref[...]], [None, None]))
    for sb, o in enumerate(_gqa_core(blocks, sink_ref)):
        o_ref[sb * tq:(sb + 1) * tq, :] = o.astype(BF16)


def _window_band_table(tq):
    off = jnp.arange(3)[:, None, None] * WA_WINDOW
    qpos = (jnp.arange(2 * tq) % tq)[None, :, None] + off
    kpos = jnp.arange(tq + 2 * WA_WINDOW)[None, None, :]
    return jnp.where(jnp.abs(qpos - kpos) <= WA_WINDOW, 0.0, NEG_INF).astype(F32)


def _gqa_attention(qkv, sink, nlat, window):
    nb, seq, _ = qkv.shape
    tq, sub_blocks = (WINDOW_TQ, WINDOW_SUB_BLOCKS) if window else (DENSE_TQ, DENSE_SUB_BLOCKS)
    step = tq * sub_blocks
    in_specs = [
        pl.BlockSpec((None, step, 256), lambda b, i: (b, i, 0)),
        pl.BlockSpec((None, seq, LANE), lambda b, i: (b, 0, 2)),
        pl.BlockSpec((None, seq, LANE), lambda b, i: (b, 0, 3)),
        pl.BlockSpec((None, CTX_LEN, LANE), lambda b, i: (nlat, b, 2)),
        pl.BlockSpec((None, CTX_LEN, LANE), lambda b, i: (nlat, b, 3)),
    ]
    args = [qkv, qkv, qkv, qkv, qkv]
    if window:
        band = _window_band_table(tq)
        in_specs = [pl.BlockSpec(memory_space=pltpu.SMEM), _const_spec(band.shape)] + in_specs
        args = [sink, band] + args
    return pl.pallas_call(
        functools.partial(_gqa_kernel, window=window, seq=seq, tq=tq, sub_blocks=sub_blocks),
        out_shape=jax.ShapeDtypeStruct((nlat, seq, BRANCH_W), BF16),
        grid=(nlat, seq // step),
        in_specs=in_specs,
        out_specs=pl.BlockSpec((None, step, BRANCH_W), lambda b, i: (b, i, 0)),
        compiler_params=_cparams(("arbitrary", "arbitrary")),
        name="window_attention" if window else "dense_attention",
    )(*args)


NA_STEP_ROWS = 4
NA_WIN_ROWS = NA_STEP_ROWS + NA_ROWS
NA_TQ = NA_STEP_ROWS * GRID_W
NA_TK = NA_WIN_ROWS * GRID_W


def _na_window_start(group, rows):
    return jnp.clip(group * NA_STEP_ROWS - NA_ROWS // 2, 0, rows - NA_WIN_ROWS)


NA_STEP_GROUPS = 4


def _na_kernel(q_ref, k_ref, v_ref, kc_ref, vc_ref, bias_ref, o_ref, *, rows):
    lo = _lane_iota((1, LANE)) < HEAD_DIM
    chains = [(gi, blk) for gi in range(NA_STEP_GROUPS) for blk in range(2)]
    starts = [pl.multiple_of(_na_window_start(pl.program_id(0) * NA_STEP_GROUPS + gi, rows) * GRID_W, GRID_W)
              for gi in range(NA_STEP_GROUPS)]
    scores = []
    for gi, blk in chains:
        cols = slice(blk * LANE, (blk + 1) * LANE)
        qb = q_ref[gi * NA_TQ:(gi + 1) * NA_TQ, cols]
        zero = jnp.zeros_like(qb)
        qs = jnp.concatenate([jnp.where(lo, qb, zero), jnp.where(lo, zero, qb)], axis=0)
        s_nb = _dot_nt(qs, k_ref[pl.ds(starts[gi], NA_TK), cols]) + bias_ref[gi, blk]
        scores.append([s_nb, _dot_nt(qs, kc_ref[:, cols])])
    outs = []
    for (gi, blk), sc in zip(chains, scores):
        cols = slice(blk * LANE, (blk + 1) * LANE)
        o = _softmax_pv(sc, [v_ref[pl.ds(starts[gi], NA_TK), cols], vc_ref[:, cols]])
        outs.append(jnp.where(lo, o[0:NA_TQ], o[NA_TQ:2 * NA_TQ]))
    for gi in range(NA_STEP_GROUPS):
        o_ref[gi * NA_TQ:(gi + 1) * NA_TQ, :] = jnp.concatenate(outs[2 * gi:2 * gi + 2], axis=1).astype(BF16)


def _na_attention(qkv, bias, layer, nlat):
    nb, seq, _ = qkv.shape
    rows = seq // GRID_W
    tq = NA_STEP_GROUPS * NA_TQ
    return pl.pallas_call(
        functools.partial(_na_kernel, rows=rows),
        out_shape=jax.ShapeDtypeStruct((nlat, seq, BRANCH_W), BF16),
        grid=(seq // tq, nlat),
        in_specs=[
            pl.BlockSpec((None, tq, 256), lambda g, b: (b, g, 0)),
            pl.BlockSpec((None, seq, 256), lambda g, b: (b, 0, 1)),
            pl.BlockSpec((None, seq, 256), lambda g, b: (b, 0, 2)),
            pl.BlockSpec((None, CTX_LEN, 256), lambda g, b: (nlat, b, 1)),
            pl.BlockSpec((None, CTX_LEN, 256), lambda g, b: (nlat, b, 2)),
            pl.BlockSpec((None, NA_STEP_GROUPS, 2, 2 * NA_TQ, NA_TK), lambda g, b: (layer, g, 0, 0, 0)),
        ],
        out_specs=pl.BlockSpec((None, tq, BRANCH_W), lambda g, b: (b, g, 0)),
        compiler_params=_cparams(("arbitrary", "arbitrary")),
        name="neighborhood_attention",
    )(qkv, qkv, qkv, qkv, qkv, bias)


def _ctx_kernel(sink_ref, qa_ref, ka_ref, va_ref, qb_ref, kb_ref, vb_ref, qc_ref, kc_ref, vc_ref,
                oa_ref, ob_ref, oc_ref):
    t = CTX_LEN
    oa_ref[...] = _gqa_core([(qa_ref[...], [ka_ref[...]], [va_ref[...]], [None])], sink_ref)[0].astype(BF16)
    oc_ref[...] = _gqa_core([(qc_ref[...], [kc_ref[...]], [vc_ref[...]], [None])], None)[0].astype(BF16)
    lo = _lane_iota((1, LANE)) < HEAD_DIM
    outs = []
    for blk in range(2):
        cols = slice(blk * LANE, (blk + 1) * LANE)
        qb = qb_ref[:, cols]
        zero = jnp.zeros_like(qb)
        qs = jnp.concatenate([jnp.where(lo, qb, zero), jnp.where(lo, zero, qb)], axis=0)
        o = _softmax_pv([_dot_nt(qs, kb_ref[:, cols])], [vb_ref[:, cols]])
        outs.append(jnp.where(lo, o[0:t], o[t:2 * t]))
    ob_ref[...] = jnp.concatenate(outs, axis=1).astype(BF16)


def _ctx_attention(sink, qa, qb, qc, nlat):
    blk = lambda w, j: pl.BlockSpec((None, CTX_LEN, w), lambda b: (nlat, b, j))
    out_spec = pl.BlockSpec((None, CTX_LEN, BRANCH_W), lambda b: (0, b, 0))
    return pl.pallas_call(
        _ctx_kernel,
        out_shape=[jax.ShapeDtypeStruct((1, nlat * CTX_LEN, BRANCH_W), BF16)] * 3,
        grid=(nlat,),
        in_specs=[
            pl.BlockSpec(memory_space=pltpu.SMEM),
            blk(256, 0), blk(LANE, 2), blk(LANE, 3),
            blk(256, 0), blk(256, 1), blk(256, 2),
            blk(256, 0), blk(LANE, 2), blk(LANE, 3),
        ],
        out_specs=[out_spec] * 3,
        compiler_params=_cparams(("arbitrary",)),
        name="context_attention",
    )(sink, qa, qa, qa, qb, qb, qb, qc, qc, qc)


CONV_PAD = 8
FIN_ROWS = 256


def _split3(t):
    hi = t.astype(BF16)
    r1 = t - hi.astype(F32)
    mid = r1.astype(BF16)
    return hi, mid, (r1 - mid.astype(F32)).astype(BF16)


LOG2_E = 1.4426950408889634
HEAD_ROWS = 16
CONV_PHASES = 4
N_SLAB = SSM_CONV_CH // LANE


def _ssd_kernel(xbc_ref, xbcc_ref, dt_ref, dtc_ref, z_ref, zc_ref, cw_ref, cb_ref, dtb_ref, alog_ref,
                dvec_ref, nrm_ref, y_ref, yc_ref, pad_ref, u_ref, ct_ref, dt8_ref, w_ref, e_ref,
                yacc_ref, loc_ref, ce_ref, *, seq):
    ltot = CTX_LEN + seq
    nchunk = ltot // SSM_CHUNK
    nctx = CTX_LEN // SSM_CHUNK
    half = SSM_CONV // 2
    cs = SSM_CHUNK
    hr = HEAD_ROWS

    def conv_into(src_ref, n, dst):
        for sl in range(N_SLAB):
            pad_ref[sl, 0:CONV_PAD, :] = jnp.zeros((CONV_PAD, LANE), F32)
            pad_ref[sl, CONV_PAD + n:2 * CONV_PAD + n, :] = jnp.zeros((CONV_PAD, LANE), F32)
            pad_ref[sl, CONV_PAD:CONV_PAD + n, :] = src_ref[:, sl * LANE:(sl + 1) * LANE]
        q = FIN_ROWS // CONV_PHASES

        def block(i, carry):
            r0 = pl.multiple_of(i * FIN_ROWS, FIN_ROWS)
            for sl in range(N_SLAB):
                lanes = slice(sl * LANE, (sl + 1) * LANE)
                for p in range(CONV_PHASES):
                    acc = jnp.broadcast_to(cb_ref[:, lanes], (q, LANE))
                    for k in range(SSM_CONV):
                        tap = pad_ref[sl, pl.ds(r0 + CONV_PAD + p + k - half, q, stride=CONV_PHASES), :]
                        acc = acc + tap * cw_ref[k:k + 1, lanes]
                    u_ref[sl, pl.ds(dst + r0 + p, q, stride=CONV_PHASES), :] = _silu(acc)
            return carry

        lax.fori_loop(0, n // FIN_ROWS, block, 0)

    conv_into(xbcc_ref, CTX_LEN, 0)
    conv_into(xbc_ref, seq, CTX_LEN)

    row = lax.broadcasted_iota(jnp.int32, (cs, cs), 0)
    col = lax.broadcasted_iota(jnp.int32, (cs, cs), 1)
    causal = (col <= row, col >= row)
    lo = _lane_iota((1, LANE)) < SSM_STATE // 2

    def softplus(t):
        return jnp.maximum(t, 0.0) + jnp.log1p(jnp.exp(-jnp.abs(t)))

    sel = jnp.where(lax.broadcasted_iota(jnp.int32, (hr, LANE), 0) == lax.broadcasted_iota(jnp.int32, (hr, LANE), 1),
                    1.0, 0.0).astype(BF16)

    def heads_to_rows(src_ref):
        return sum(_dot_nt(sel, part) for part in _split3(src_ref[...]))

    dt_t = softplus(jnp.concatenate([heads_to_rows(dtc_ref), heads_to_rows(dt_ref)], axis=1) + dtb_ref[...])
    da_t = dt_t * (-jnp.exp(alog_ref[...]))

    def by_chunk(t):
        return jnp.concatenate([t[:, c * cs:(c + 1) * cs] for c in range(nchunk)], axis=0)

    dt_c, da_c = by_chunk(dt_t), by_chunk(da_t)
    tri = jnp.where(row <= col, 1.0, 0.0).astype(BF16)
    pre = sum(_dot(part, tri) for part in _split3(da_c))
    is_fwd = lax.broadcasted_iota(jnp.int32, (nchunk * hr, 1), 0) % hr < SSM_HEADS
    ct = jnp.where(is_fwd, pre, (pre[:, cs - 1:cs] - pre) + da_c) * LOG2_E
    tot = jnp.where(is_fwd, ct[:, cs - 1:cs], ct[:, 0:1])
    ct_ref[...] = ct
    dt8_ref[...] = dt_c
    w_ref[...] = dt_c * jnp.exp2(tot - ct)
    e_ref[...] = jnp.broadcast_to(jnp.exp2(tot), (nchunk * hr, cs))

    def chunk_rows(c):
        return pl.ds(pl.multiple_of(c * cs, cs), cs)

    def head_rows(c):
        return pl.ds(pl.multiple_of(c * hr, hr), hr)

    def phase1(c, carry):
        rows = chunk_rows(c)
        ct8 = ct_ref[head_rows(c), :]
        dt8 = dt8_ref[head_rows(c), :]
        w8 = w_ref[head_rows(c), :]
        for g in range(SSM_GROUPS):
            gl = slice(g * LANE, (g + 1) * LANE)
            xg = u_ref[g, rows, :]
            bg = u_ref[SSM_GROUPS + g, rows, :]
            cg = u_ref[2 * SSM_GROUPS + g, rows, :]
            xcat = jnp.concatenate([jnp.where(lo, xg, 0.0), jnp.where(lo, 0.0, xg)], axis=0).astype(BF16)
            bt = bg.T
            cb = _dot_nt(cg.astype(BF16), bg.astype(BF16))
            mh = [None, None]
            ces, bts = [], []
            for d in range(2):
                for hh in range(2):
                    j = d * SSM_HEADS + 2 * g + hh
                    cum_j = jnp.broadcast_to(ct8[j:j + 1, :], (cs, cs)).T
                    seg = jnp.where(causal[d], cum_j - ct8[j:j + 1, :], NEG_INF)
                    term = jnp.exp2(seg) * dt8[j:j + 1, :]
                    mh[hh] = term if d == 0 else mh[hh] + term
                    ces.append((cg * jnp.exp2(cum_j)).astype(BF16))
                    bts.append((bt * w8[j:j + 1, :]).astype(BF16))
            m = jnp.concatenate([(cb * mh[0]).astype(BF16), (cb * mh[1]).astype(BF16)], axis=1)
            yacc_ref[rows, gl] = _dot(m, xcat)
            ce_ref[rows, g * 4 * LANE:(g + 1) * 4 * LANE] = jnp.concatenate(ces, axis=1)
            for d in range(2):
                st = _dot(jnp.concatenate([bts[2 * d], bts[2 * d + 1]], axis=1), xcat)
                loc_ref[c, :, d * SSM_INNER + g * LANE:d * SSM_INNER + (g + 1) * LANE] = st
        return carry

    lax.fori_loop(0, nchunk, phase1, 0, unroll=2)

    def scan(order, d):
        cols = slice(d * SSM_INNER, (d + 1) * SSM_INNER)

        def body(t, s):
            c = order(t)
            local = loc_ref[c, :, cols]
            loc_ref[c, :, cols] = s
            e8 = e_ref[head_rows(c), :]
            j0 = d * SSM_HEADS
            decay = jnp.concatenate([jnp.where(lo, e8[j0 + 2 * g:j0 + 2 * g + 1, :], e8[j0 + 2 * g + 1:j0 + 2 * g + 2, :])
                                     for g in range(SSM_GROUPS)], axis=1)
            return s * decay + local
        lax.fori_loop(0, nchunk, body, jnp.zeros((SSM_STATE, SSM_INNER), F32))

    scan(lambda t: t, 0)
    scan(lambda t: jnp.where(t < nctx, nctx - 1 - t, nchunk + nctx - 1 - t), 1)

    def finish_chunk(c, z_rows, out_ref, out_rows):
        rows = chunk_rows(c)
        s_in = loc_ref[c]
        ys = []
        for g in range(SSM_GROUPS):
            gl = slice(g * LANE, (g + 1) * LANE)
            sf = s_in[:, gl]
            sb = s_in[:, SSM_INNER + g * LANE:SSM_INNER + (g + 1) * LANE]
            scat = jnp.concatenate([jnp.where(lo, sf, 0.0), jnp.where(lo, 0.0, sf),
                                    jnp.where(lo, sb, 0.0), jnp.where(lo, 0.0, sb)], axis=0).astype(BF16)
            ys.append(yacc_ref[rows, gl] + _dot(ce_ref[rows, g * 4 * LANE:(g + 1) * 4 * LANE], scat))
        xs = jnp.concatenate([u_ref[g, rows, :] for g in range(SSM_GROUPS)], axis=1)
        y = jnp.concatenate(ys, axis=1) + dvec_ref[...] * xs
        y = y * _silu(z_rows)
        out_ref[out_rows, :] = _rms(y, nrm_ref[...]).astype(BF16)

    for c in range(nctx):
        finish_chunk(c, zc_ref[c * cs:(c + 1) * cs, :], yc_ref, slice(c * cs, (c + 1) * cs))

    def finish_body(cl, carry):
        out_rows = chunk_rows(cl)
        finish_chunk(cl + nctx, z_ref[out_rows, :], y_ref, out_rows)
        return carry

    lax.fori_loop(0, nchunk - nctx, finish_body, 0, unroll=2)


def _ssd(xbc, dt, z, conv_w, conv_b, dt_bias, a_log, dvec, nrm, nlat):
    nb, seq, _ = xbc.shape
    ltot = CTX_LEN + seq
    nchunk = ltot // SSM_CHUNK
    lat = lambda w: pl.BlockSpec((None, seq, w), lambda b: (b, 0, 0))
    ctx = lambda w: pl.BlockSpec((None, CTX_LEN, w), lambda b: (nlat, b, 0))
    return pl.pallas_call(
        functools.partial(_ssd_kernel, seq=seq),
        out_shape=[jax.ShapeDtypeStruct((nlat, seq, SSM_INNER), BF16),
                   jax.ShapeDtypeStruct((nlat, CTX_LEN, SSM_INNER), BF16)],
        grid=(nlat,),
        in_specs=[
            lat(SSM_CONV_CH), ctx(SSM_CONV_CH), lat(DT_PAD), ctx(DT_PAD), lat(SSM_INNER), ctx(SSM_INNER),
            _const_spec((8, SSM_CONV_CH)), _const_spec((1, SSM_CONV_CH)),
            _const_spec((HEAD_ROWS, 1)), _const_spec((HEAD_ROWS, 1)),
            _const_spec((1, SSM_INNER)), _const_spec((1, SSM_INNER)),
        ],
        out_specs=[pl.BlockSpec((None, seq, SSM_INNER), lambda b: (b, 0, 0)),
                   pl.BlockSpec((None, CTX_LEN, SSM_INNER), lambda b: (b, 0, 0))],
        scratch_shapes=[
            pltpu.VMEM((N_SLAB, seq + 2 * CONV_PAD, LANE), F32),
            pltpu.VMEM((N_SLAB, ltot, LANE), F32),
            pltpu.VMEM((nchunk * HEAD_ROWS, SSM_CHUNK), F32),
            pltpu.VMEM((nchunk * HEAD_ROWS, SSM_CHUNK), F32),
            pltpu.VMEM((nchunk * HEAD_ROWS, SSM_CHUNK), F32),
            pltpu.VMEM((nchunk * HEAD_ROWS, SSM_CHUNK), F32),
            pltpu.VMEM((ltot, SSM_INNER), F32),
            pltpu.VMEM((nchunk, SSM_STATE, 2 * SSM_INNER), F32),
            pltpu.VMEM((ltot, 2 * SSM_GROUPS * 2 * SSM_STATE), BF16),
        ],
        compiler_params=_cparams(("arbitrary",)),
        name="ssd",
    )(xbc, xbc, dt, dt, z, z, conv_w, conv_b, dt_bias, a_log, dvec, nrm)


MERGE_TM = 512


def _merge_kernel(*refs, nlat_split):
    x_ref, sh_ref, sc_ref, gt_ref, g_ref = refs[:5]
    y_refs = refs[5:9]
    yc_refs = refs[9:13] if nlat_split is not None else None
    wgl_ref, bg_ref, wbr_ref, wo_ref, o_ref = refs[-5:]
    x = x_ref[...]
    xn = _ada(x, g_ref[...], sh_ref[...], sc_ref[...]).astype(BF16)
    mix = None
    for k in range(N_BRANCH):
        cols = slice(k * D_MODEL, (k + 1) * D_MODEL)
        pre = _dot(xn, wgl_ref[:, cols]) + bg_ref[:, cols]
        gate = 1.0 / (1.0 + jnp.exp(-pre))
        y = y_refs[k][...]
        if nlat_split is not None:
            y = jnp.where(pl.program_id(0) == nlat_split, yc_refs[k][...], y)
        term = gate * _dot(y, wbr_ref[k])
        mix = term if mix is None else mix + term
    o_ref[...] = x + gt_ref[...] * _dot(mix.astype(BF16), wo_ref[...])


def _merge(x_all, mod_l, gain, ys, ys_ctx, w_gate, b_gate, w_branch, w_out, layer, nb):
    seq = x_all.shape[1]
    nlat = ys[0].shape[0]
    tok = lambda w: pl.BlockSpec((None, MERGE_TM, w), lambda b, t: (b, t, 0))
    lat = pl.BlockSpec((None, MERGE_TM, BRANCH_W), lambda b, t: (jnp.minimum(b, nlat - 1), t, 0))
    in_specs = [tok(D_MODEL), _mod_spec(3), _mod_spec(4), _mod_spec(5), _const_spec((1, D_MODEL))] + [lat] * 4
    args = [x_all, mod_l, mod_l, mod_l, gain.reshape(1, D_MODEL), *ys]
    if ys_ctx is not None:
        in_specs += [pl.BlockSpec((None, MERGE_TM, BRANCH_W), lambda b, t: (0, t, 0))] * 4
        args += list(ys_ctx)
    in_specs += [
        _layer_spec((D_MODEL, N_BRANCH * D_MODEL), layer), _layer_spec((1, N_BRANCH * D_MODEL), layer),
        _layer_spec((N_BRANCH, BRANCH_W, D_MODEL), layer), _layer_spec((D_MODEL, D_MODEL), layer),
    ]
    args += [w_gate, b_gate.reshape(DEPTH, 1, N_BRANCH * D_MODEL), w_branch, w_out]
    return pl.pallas_call(
        functools.partial(_merge_kernel, nlat_split=nlat if ys_ctx is not None else None),
        out_shape=jax.ShapeDtypeStruct((nb, seq, D_MODEL), F32),
        grid=(nb, seq // MERGE_TM),
        in_specs=in_specs,
        out_specs=tok(D_MODEL),
        compiler_params=_cparams(("arbitrary", "arbitrary")),
        name="merge",
    )(*args)


def _rope_table(seq):
    pos = jnp.arange(seq)
    axes = jnp.stack([pos // GRID_W, pos % GRID_W], axis=-1).astype(F32)
    quarter = HEAD_DIM // 4
    inv = 1.0 / (ROPE_THETA ** (jnp.arange(quarter, dtype=F32) * 4.0 / HEAD_DIM))
    ang = axes[:, :, None] * inv
    cos, sin = jnp.cos(ang), jnp.sin(ang)
    cos_h = jnp.concatenate([cos, cos], axis=-1).reshape(seq, HEAD_DIM)
    sin_h = jnp.concatenate([-sin, sin], axis=-1).reshape(seq, HEAD_DIM)
    lat = jnp.concatenate([jnp.tile(cos_h, (1, 4)), jnp.tile(sin_h, (1, 4))], axis=-1)
    ident = jnp.concatenate([jnp.ones((seq, 256), F32), jnp.zeros((seq, 256), F32)], axis=-1)
    return jnp.stack([lat, ident])


def _permute_heads(w, axis):
    parts = jnp.split(w, 4, axis=axis)
    return jnp.concatenate([parts[j] for j in Q_HEAD_ORDER], axis=axis)


CAST_SPLIT = 4


def _cast_kernel(w_ref, o_ref):
    o_ref[...] = w_ref[...].astype(BF16)


def _cast_bf16(w):
    depth, r, c = w.shape
    tr = r // CAST_SPLIT
    spec = pl.BlockSpec((None, tr, c), lambda l, i: (l, i, 0))
    return pl.pallas_call(
        _cast_kernel,
        out_shape=jax.ShapeDtypeStruct(w.shape, BF16),
        grid=(depth, CAST_SPLIT),
        in_specs=[spec], out_specs=spec,
        compiler_params=_cparams(("arbitrary", "arbitrary")),
        name="cast_bf16",
    )(w)


MXU_TILE = 256
GATE_COL0 = PDT + 2 * SSM_HEADS


def _transpose_cast_kernel(w_ref, o_ref):
    eye = jnp.where(lax.broadcasted_iota(jnp.int32, (MXU_TILE, MXU_TILE), 0)
                    == lax.broadcasted_iota(jnp.int32, (MXU_TILE, MXU_TILE), 1), 1.0, 0.0).astype(BF16)
    for k0 in range(0, D_MODEL, MXU_TILE):
        blk = w_ref[:, k0:k0 + MXU_TILE].astype(BF16)
        o_ref[k0:k0 + MXU_TILE, :] = _dot_nt(eye, blk).astype(BF16)


def _pack_w_in(w_in_t):
    depth = w_in_t.shape[0]
    proj = pl.pallas_call(
        _transpose_cast_kernel,
        out_shape=jax.ShapeDtypeStruct((depth, D_MODEL, PROJ_N), BF16),
        grid=(depth,),
        in_specs=[pl.BlockSpec((None, PROJ_N, D_MODEL), lambda l: (l, 0, 0))],
        out_specs=pl.BlockSpec((None, D_MODEL, PROJ_N), lambda l: (l, 0, 0)),
        compiler_params=_cparams(("arbitrary",)),
        name="pack_proj",
    )(w_in_t)
    gate = pl.pallas_call(
        lambda w_ref, o_ref: _transpose_cast_kernel(w_ref.at[0], o_ref),
        out_shape=jax.ShapeDtypeStruct((depth, D_MODEL, N_BRANCH * D_MODEL), BF16),
        grid=(depth, N_BRANCH),
        in_specs=[pl.BlockSpec((pl.Element(1), pl.Element(D_MODEL), pl.Element(D_MODEL)),
                               lambda l, k: (l, pl.multiple_of(GATE_COL0 + k * D_MODEL, 8), 0))],
        out_specs=pl.BlockSpec((None, D_MODEL, D_MODEL), lambda l, k: (l, 0, k)),
        compiler_params=_cparams(("arbitrary", "arbitrary")),
        name="pack_gate",
    )(w_in_t)
    return proj, gate


NA_BANDS = 2 * NA_ROWS - 1


def _na_bias_kernel(rpb_ref, o_ref, band_ref, *, rows):
    group = pl.program_id(1)
    c = lax.broadcasted_iota(jnp.int32, (GRID_W, LANE), 0)
    kc = lax.broadcasted_iota(jnp.int32, (GRID_W, LANE), 1)
    lo = kc < GRID_W

    @pl.when(group == 0)
    def _():
        qstart = jnp.clip(c - NA_COLS // 2, 0, GRID_W - NA_COLS)
        ok = (kc >= qstart) & (kc < qstart + NA_COLS)
        for h in range(2 * 2):
            for dr in range(NA_BANDS):
                line = jnp.broadcast_to(rpb_ref[h, dr:dr + 1, :], (GRID_W, LANE))
                skew = pltpu.roll(line, LANE - (NA_COLS - 1), 1, stride=1, stride_axis=0)
                band_ref[h, dr] = jnp.where(ok, skew, NEG_INF)

    w0 = _na_window_start(group, rows)
    for h in range(2 * 2):
        blk, g = divmod(h, 2)
        for qi in range(NA_STEP_ROWS):
            r = group * NA_STEP_ROWS + qi
            rs = jnp.clip(r - NA_ROWS // 2, 0, rows - NA_ROWS)
            for m in range(NA_WIN_ROWS // 2):
                halves = []
                for kr in (w0 + 2 * m, w0 + 2 * m + 1):
                    in_window = (kr >= rs) & (kr < rs + NA_ROWS)
                    dr = jnp.clip(kr - r + NA_ROWS - 1, 0, NA_BANDS - 1)
                    halves.append(jnp.where(in_window, band_ref[h, dr], NEG_INF))
                pair = jnp.where(lo, halves[0], pltpu.roll(halves[1], GRID_W, 1))
                o_ref[blk, g * NA_TQ + qi * GRID_W:g * NA_TQ + (qi + 1) * GRID_W, m * LANE:(m + 1) * LANE] = pair


def _na_bias_table(rpb, rows):
    depth, heads = rpb.shape[:2]
    groups = rows // NA_STEP_ROWS
    padded = jnp.pad(rpb, ((0, 0), (0, 0), (0, 16 - rpb.shape[2]), (0, LANE - rpb.shape[3])))
    return pl.pallas_call(
        functools.partial(_na_bias_kernel, rows=rows),
        out_shape=jax.ShapeDtypeStruct((depth, groups, 2, 2 * NA_TQ, NA_TK), F32),
        grid=(depth, groups),
        in_specs=[pl.BlockSpec((None, heads, 16, LANE), lambda l, g: (l, 0, 0, 0))],
        out_specs=pl.BlockSpec((None, None, 2, 2 * NA_TQ, NA_TK), lambda l, g: (l, g, 0, 0, 0)),
        scratch_shapes=[pltpu.VMEM((heads, 16, GRID_W, LANE), F32)],
        compiler_params=_cparams(("arbitrary", "arbitrary")),
        name="na_bias",
    )(padded)


def kernel(x, c, ctx, c_ctx, w_mod, b_mod, norm_ffn1, ffn1_w_gate, ffn1_w_up, ffn1_w_down, norm_mix, w_in, b_gate,
           attn_sink, na_rpb, qk_norm_q, qk_norm_k, ssm_conv_w, ssm_conv_b, ssm_dt_bias, ssm_a_log, ssm_d, ssm_norm,
           w_branch, w_out, norm_ffn2, ffn2_w_gate, ffn2_w_up, ffn2_w_down, final_norm):
    nlat, seq, _ = x.shape
    assert nlat * CTX_LEN == seq and nlat + 1 <= MOD_ROWS
    nb = nlat + 1
    c_rows = jnp.concatenate([c, c_ctx[None], jnp.zeros((MOD_ROWS - nb, D_MODEL), F32)], axis=0)
    mod = _modulation(c_rows, w_mod, b_mod).reshape(DEPTH, N_MOD, MOD_ROWS, 1, D_MODEL)
    rope_tab = _rope_table(seq)
    head_col = lambda v: jnp.pad(v.reshape(-1, 1), ((0, HEAD_ROWS - v.size), (0, 0)))

    ffn1_w = [_cast_bf16(w) for w in (ffn1_w_gate, ffn1_w_up, ffn1_w_down)]
    ffn2_w = [_cast_bf16(w) for w in (ffn2_w_gate, ffn2_w_up, ffn2_w_down)]
    w_proj, w_gl = _pack_w_in(jnp.swapaxes(w_in, 1, 2))
    w_o = _cast_bf16(w_out)
    wbr = jnp.stack([_permute_heads(w_branch[:, 0], 1), w_branch[:, 1],
                     _permute_heads(w_branch[:, 2], 1), w_branch[:, 3]], axis=1).astype(BF16)
    na_bias = _na_bias_table(na_rpb, seq // GRID_W)

    for l in range(DEPTH):
        with_ctx = l < DEPTH - 1
        last = l == DEPTH - 1
        nq = jnp.tile(qk_norm_q[l], 4).reshape(1, 256)
        nk = jnp.tile(qk_norm_k[l], 4).reshape(1, 256)
        conv_w = jnp.pad(ssm_conv_w[l], ((0, 8 - SSM_CONV), (0, 0)))
        dvec = jnp.repeat(ssm_d[l], SSM_INNER // SSM_HEADS).reshape(1, SSM_INNER)

        if l == 0:
            x_all = _ffn(x, mod[l], (0, 1, 2), norm_ffn1[l], *ffn1_w, l, nb, x_ctx=ctx.reshape(1, seq, D_MODEL))
        else:
            x_all = _ffn(x_all, mod[l], (0, 1, 2), norm_ffn1[l], *ffn1_w, l, nb)
        qa, qb, qc, z, xbc, dt = _proj(x_all, mod[l], norm_mix[l], w_proj, l, rope_tab, nq, nk, nlat)
        ya = _gqa_attention(qa, attn_sink[l], nlat, window=True)
        yb = _na_attention(qb, na_bias, l, nlat)
        yc = _gqa_attention(qc, None, nlat, window=False)
        yd, yd_ctx = _ssd(xbc, dt, z, conv_w, ssm_conv_b[l].reshape(1, -1), head_col(ssm_dt_bias[l]),
                          head_col(ssm_a_log[l]), dvec, ssm_norm[l].reshape(1, -1), nlat)
        nbm = nb if with_ctx else nlat
        ys_ctx = None
        if with_ctx:
            ys_ctx = (*_ctx_attention(attn_sink[l], qa, qb, qc, nlat), yd_ctx.reshape(1, seq, SSM_INNER))
        x_all = _merge(x_all, mod[l], norm_mix[l], (ya, yb, yc, yd), ys_ctx, w_gl, b_gate, wbr, w_o, l, nbm)
        x_all = _ffn(x_all, mod[l], (6, 7, 8), norm_ffn2[l], *ffn2_w, l, nbm,
                     final_gain=final_norm if last else None)
    return x_all
```

```python
import functools

import jax
import jax.numpy as jnp
from jax import lax
from jax.experimental import pallas as pl
from jax.experimental.pallas import tpu as pltpu

D_MODEL = 1024
DEPTH = 4
CTX_LEN = 256
GRID_W = 64
HEAD_DIM = 64
ROPE_THETA = 10000.0
EPS = 1e-6
NEG_INF = -1e30
D_FF = 2816
N_BRANCH = 4
BRANCH_W = 256
WA_WINDOW = 128
NA_ROWS = 8
NA_COLS = 16
SSM_HEADS = 4
SSM_GROUPS = 2
SSM_STATE = 128
SSM_CONV = 5
SSM_CHUNK = 128
SSM_INNER = 256
SSM_CONV_CH = 768
N_MOD = 9
MOD_ROWS = 16
LANE = 128
DT_PAD = LANE
QK_SCALE = HEAD_DIM ** -0.5
VMEM_LIMIT = 56 * 1024 * 1024

BF16 = jnp.bfloat16
F32 = jnp.float32


def _cparams(sem):
    return pltpu.CompilerParams(dimension_semantics=sem, vmem_limit_bytes=VMEM_LIMIT)


def _const_spec(shape):
    nd = len(shape)
    return pl.BlockSpec(shape, lambda *_: (0,) * nd)


def _layer_spec(shape, layer):
    nd = len(shape)
    return pl.BlockSpec((None,) + tuple(shape), lambda *_: (layer,) + (0,) * nd)


def _dot(a, b):
    return jnp.dot(a, b, preferred_element_type=F32)


def _dot_nt(a, b):
    return lax.dot_general(a, b, (((1,), (1,)), ((), ())), preferred_element_type=F32)


def _silu(t):
    return t * (1.0 / (1.0 + jnp.exp(-t)))


def _rms(x, g):
    return (x * lax.rsqrt(jnp.mean(x * x, axis=-1, keepdims=True) + EPS)) * g


def _ada(x, g, shift, scale):
    return _rms(x, g) * (1.0 + scale) + shift


def _mod_kernel(c_ref, w_ref, b_ref, o_ref):
    act = _silu(c_ref[...]).astype(BF16)
    o_ref[...] = _dot(act, w_ref[...].astype(BF16)) + b_ref[...]


def _modulation(c_rows, w_mod, b_mod):
    return pl.pallas_call(
        _mod_kernel,
        out_shape=jax.ShapeDtypeStruct((DEPTH, N_MOD, MOD_ROWS, D_MODEL), F32),
        grid=(DEPTH, N_MOD),
        in_specs=[
            pl.BlockSpec((MOD_ROWS, D_MODEL), lambda l, i: (0, 0)),
            pl.BlockSpec((None, D_MODEL, D_MODEL), lambda l, i: (l, 0, i)),
            pl.BlockSpec((None, 1, D_MODEL), lambda l, i: (l, 0, i)),
        ],
        out_specs=pl.BlockSpec((None, None, MOD_ROWS, D_MODEL), lambda l, i: (l, i, 0, 0)),
        compiler_params=_cparams(("arbitrary", "arbitrary")),
        name="modulation",
    )(c_rows, w_mod, b_mod.reshape(DEPTH, 1, N_MOD * D_MODEL))


def _mod_spec(kind):
    return pl.BlockSpec((None, None, 1, D_MODEL), lambda b, t: (kind, b, 0, 0))


FFN_TM = 512
FFN_CHUNKS = ((0, 1024), (1024, 2048), (2048, 2816))


FFN_STAGE_ROWS = 256


def _stream_cast(src, dst, stage, sem):
    rows = src.shape[0]
    nblk = rows // FFN_STAGE_ROWS

    def copy(i):
        return pltpu.make_async_copy(src.at[pl.ds(i * FFN_STAGE_ROWS, FFN_STAGE_ROWS), :], stage.at[i % 2],
                                     sem.at[i % 2])

    copy(0).start()
    for i in range(nblk):
        if i + 1 < nblk:
            copy(i + 1).start()
        copy(i).wait()
        dst[i * FFN_STAGE_ROWS:(i + 1) * FFN_STAGE_ROWS, :] = stage[i % 2].astype(BF16)


def _ffn_kernel(*refs, final, nlat_split, layer):
    refs = list(refs)
    x_ref = refs.pop(0)
    xc_ref = refs.pop(0) if nlat_split is not None else None
    sh_ref, sc_ref, gt_ref, g_ref, wg_hbm, wu_hbm, wd_hbm = refs[:7]
    fg_ref = refs[7] if final else None
    o_ref, wg_ref, wu_ref, wd_ref, stage_in, stage_out, sem = refs[-7:]

    @pl.when((pl.program_id(0) == 0) & (pl.program_id(1) == 0))
    def _():
        _stream_cast(wg_hbm.at[layer], wg_ref, stage_in, sem)
        _stream_cast(wu_hbm.at[layer], wu_ref, stage_in, sem)
        _stream_cast(wd_hbm.at[layer], wd_ref, stage_out, sem)

    x = x_ref[...]
    if nlat_split is not None:
        x = jnp.where(pl.program_id(0) == nlat_split, xc_ref[...], x)
    xn = _ada(x, g_ref[...], sh_ref[...], sc_ref[...]).astype(BF16)
    acc = None
    for c0, c1 in FFN_CHUNKS:
        gate = _dot(xn, wg_ref[:, c0:c1])
        up = _dot(xn, wu_ref[:, c0:c1])
        h = (_silu(gate) * up).astype(BF16)
        part = _dot(h, wd_ref[c0:c1, :])
        acc = part if acc is None else acc + part
    y = x + (0.5 * gt_ref[...]) * acc
    if final:
        y = _rms(y, fg_ref[...])
    o_ref[...] = y


def _ffn(x_all, mod_l, kinds, gain, wg, wu, wd, layer, nb, final_gain=None, x_ctx=None):
    seq = x_all.shape[1]
    hbm = pl.BlockSpec(memory_space=pl.ANY)
    final = final_gain is not None
    tok = pl.BlockSpec((None, FFN_TM, D_MODEL), lambda b, t: (b, t, 0))
    if x_ctx is None:
        nlat_split = None
        in_specs, args = [tok], [x_all]
    else:
        nlat_split = x_all.shape[0]
        in_specs = [pl.BlockSpec((None, FFN_TM, D_MODEL), lambda b, t: (jnp.minimum(b, nlat_split - 1), t, 0)),
                    pl.BlockSpec((None, FFN_TM, D_MODEL), lambda b, t: (0, t, 0))]
        args = [x_all, x_ctx]
    in_specs += [
        _mod_spec(kinds[0]), _mod_spec(kinds[1]), _mod_spec(kinds[2]),
        _const_spec((1, D_MODEL)),
        hbm, hbm, hbm,
    ]
    args += [mod_l, mod_l, mod_l, gain.reshape(1, D_MODEL), wg, wu, wd]
    if final:
        in_specs.append(_const_spec((1, D_MODEL)))
        args.append(final_gain.reshape(1, D_MODEL))
    return pl.pallas_call(
        functools.partial(_ffn_kernel, final=final, nlat_split=nlat_split, layer=layer),
        out_shape=jax.ShapeDtypeStruct((nb, seq, D_MODEL), F32),
        grid=(nb, seq // FFN_TM),
        in_specs=in_specs,
        out_specs=pl.BlockSpec((None, FFN_TM, D_MODEL), lambda b, t: (b, t, 0)),
        scratch_shapes=[
            pltpu.VMEM((D_MODEL, D_FF), BF16), pltpu.VMEM((D_MODEL, D_FF), BF16), pltpu.VMEM((D_FF, D_MODEL), BF16),
            pltpu.VMEM((2, FFN_STAGE_ROWS, D_FF), F32), pltpu.VMEM((2, FFN_STAGE_ROWS, D_MODEL), F32),
            pltpu.SemaphoreType.DMA((2,)),
        ],
        compiler_params=_cparams(("arbitrary", "arbitrary")),
        name="ffn",
    )(*args)


PA, PB, PC, PZ, PX, PDT = 0, 512, 1280, 1792, 2048, 2816
PROJ_N = PDT + DT_PAD
PROJ_TM = 512
Q_HEAD_ORDER = (0, 2, 1, 3)


def _lane_iota(shape):
    return lax.broadcasted_iota(jnp.int32, shape, len(shape) - 1)


def _rope(t, cos, sin):
    w = t.shape[-1]
    first = (_lane_iota((1, w)) % 32) < 16
    rot = jnp.where(first, pltpu.roll(t, w - 16, 1), pltpu.roll(t, 16, 1))
    return t * cos[:, :w] + rot * sin[:, :w]


def _head_norm(t, gain):
    w = t.shape[-1]
    lane = _lane_iota((1, w))
    sq = t * t
    scale = jnp.zeros_like(t)
    for h in range(w // HEAD_DIM):
        m = (lane >= h * HEAD_DIM) & (lane < (h + 1) * HEAD_DIM)
        ms = jnp.sum(jnp.where(m, sq, 0.0), axis=-1, keepdims=True) * (1.0 / HEAD_DIM)
        scale = jnp.where(m, lax.rsqrt(ms + EPS), scale)
    return (t * scale) * gain[:, :w]


def _pair_heads(t):
    lo = _lane_iota((1, LANE)) < HEAD_DIM
    t0, t1 = t[:, 0:LANE], t[:, LANE:2 * LANE]
    return jnp.concatenate([jnp.where(lo, t0, pltpu.roll(t1, HEAD_DIM, 1)),
                            jnp.where(lo, pltpu.roll(t0, HEAD_DIM, 1), t1)], axis=1)


def _proj_kernel(x_ref, sh_ref, sc_ref, g_ref, w_ref, rope_ref, nq_ref, nk_ref,
                 qa_ref, qb_ref, qc_ref, z_ref, xbc_ref, dt_ref):
    xn = _ada(x_ref[...], g_ref[...], sh_ref[...], sc_ref[...]).astype(BF16)
    cos = rope_ref[:, 0:256]
    sin = rope_ref[:, 256:512]
    a = _dot(xn, w_ref[:, PA:PA + 512])
    qa_ref[:, 0:256] = _pair_heads(_rope(a[:, 0:256], cos, sin) * QK_SCALE).astype(BF16)
    qa_ref[:, 256:384] = _rope(a[:, 256:384], cos, sin).astype(BF16)
    qa_ref[:, 384:512] = a[:, 384:512].astype(BF16)
    b = _dot(xn, w_ref[:, PB:PB + 768])
    qb_ref[:, 0:256] = (b[:, 0:256] * QK_SCALE).astype(BF16)
    qb_ref[:, 256:768] = b[:, 256:768].astype(BF16)
    c = _dot(xn, w_ref[:, PC:PC + 512])
    qc_ref[:, 0:256] = _pair_heads(_rope(_head_norm(c[:, 0:256], nq_ref[...]), cos, sin) * QK_SCALE).astype(BF16)
    qc_ref[:, 256:384] = _rope(_head_norm(c[:, 256:384], nk_ref[...]), cos, sin).astype(BF16)
    qc_ref[:, 384:512] = c[:, 384:512].astype(BF16)
    z_ref[...] = _dot(xn, w_ref[:, PZ:PZ + 256])
    xbc_ref[...] = _dot(xn, w_ref[:, PX:PX + 768])
    dt_ref[...] = _dot(xn, w_ref[:, PDT:PDT + DT_PAD])


def _proj(x_all, mod_l, gain, w_proj, layer, rope_tab, nq, nk, nlat):
    nb, seq, _ = x_all.shape
    tok = lambda w: pl.BlockSpec((None, PROJ_TM, w), lambda b, t: (b, t, 0))
    outs = [(512, BF16), (768, BF16), (512, BF16), (256, F32), (768, F32), (DT_PAD, F32)]
    return pl.pallas_call(
        _proj_kernel,
        out_shape=[jax.ShapeDtypeStruct((nb, seq, w), dt) for w, dt in outs],
        grid=(nb, seq // PROJ_TM),
        in_specs=[
            tok(D_MODEL), _mod_spec(3), _mod_spec(4),
            _const_spec((1, D_MODEL)),
            _layer_spec((D_MODEL, PROJ_N), layer),
            pl.BlockSpec((None, PROJ_TM, 512), lambda b, t: (b // nlat, t, 0)),
            _const_spec((1, 256)), _const_spec((1, 256)),
        ],
        out_specs=[tok(w) for w, _ in outs],
        compiler_params=_cparams(("arbitrary", "arbitrary")),
        name="in_proj",
    )(x_all, mod_l, mod_l, gain.reshape(1, D_MODEL), w_proj, rope_tab, nq, nk)


def _gqa_core(blocks, sink_ref):
    lo = _lane_iota((1, LANE)) < HEAD_DIM
    chain_scores = []
    for q, keys, _, biases in blocks:
        t0, t1 = q[:, 0:LANE], q[:, LANE:2 * LANE]
        zero = jnp.zeros_like(t0)
        for hk in range(2):
            if hk == 0:
                qs = jnp.concatenate([jnp.where(lo, t0, zero), jnp.where(lo, t1, zero)], axis=0)
            else:
                qs = jnp.concatenate([jnp.where(lo, zero, t0), jnp.where(lo, zero, t1)], axis=0)
            scores = []
            for k, bias in zip(keys, biases):
                s = _dot_nt(qs, k)
                scores.append(s if bias is None else s + bias)
            chain_scores.append(scores)
    results = []
    for sb, (q, _, values, _) in enumerate(blocks):
        t = q.shape[0]
        outs = []
        for hk in range(2):
            extra = None
            if sink_ref is not None:
                row = lax.broadcasted_iota(jnp.int32, (2 * t, 1), 0)
                extra = jnp.where(row < t, sink_ref[2 * hk], sink_ref[2 * hk + 1])
            outs.append(_softmax_pv(chain_scores[2 * sb + hk], values, extra, spare_lanes=~lo if hk == 0 else lo))
        o0, o1 = outs
        results.append(jnp.concatenate([jnp.where(lo, o0[0:t], o1[0:t]),
                                        jnp.where(lo, o0[t:2 * t], o1[t:2 * t])], axis=1))
    return results


def _softmax_pv(scores, values, extra_logit=None, spare_lanes=None):
    m = None
    for s in scores:
        sm = jnp.max(s, axis=-1, keepdims=True)
        m = sm if m is None else jnp.maximum(m, sm)
    if extra_logit is not None:
        m = jnp.maximum(m, extra_logit)
    acc = None
    for s, v in zip(scores, values):
        p = jnp.exp((s - m).astype(BF16))
        if spare_lanes is None:
            pv = _dot(p, jnp.concatenate([v, jnp.ones_like(v)], axis=1))
        else:
            pv = _dot(p, jnp.where(spare_lanes, jnp.ones_like(v), v))
        acc = pv if acc is None else acc + pv
    if spare_lanes is None:
        den = acc[:, LANE:2 * LANE]
        acc = acc[:, 0:LANE]
    else:
        den = pltpu.roll(acc, HEAD_DIM, 1)
    if extra_logit is not None:
        den = den + jnp.exp(extra_logit - m)
    return acc * (1.0 / den)


WINDOW_TQ = 256
WINDOW_SUB_BLOCKS = 4
DENSE_TQ = 128
DENSE_SUB_BLOCKS = 4


def _gqa_kernel(*refs, window, seq, tq, sub_blocks):
    if window:
        sink_ref, band_ref, q_ref, k_ref, v_ref, kc_ref, vc_ref, o_ref = refs
    else:
        q_ref, k_ref, v_ref, kc_ref, vc_ref, o_ref = refs
        sink_ref = None
    blocks = []
    for sb in range(sub_blocks):
        q = q_ref[sb * tq:(sb + 1) * tq, :]
        if window:
            nkw = tq + 2 * WA_WINDOW
            q0 = (pl.program_id(1) * sub_blocks + sb) * tq
            start = pl.multiple_of(jnp.clip(q0 - WA_WINDOW, 0, seq - nkw), LANE)
            bias = band_ref[(q0 - start) // WA_WINDOW]
            blocks.append((q, [k_ref[pl.ds(start, nkw), :], kc_ref[...]], [v_ref[pl.ds(start, nkw), :], vc_ref[...]],
                           [bias, None]))
        else:
            blocks.append((q, [k_ref[...], kc_ref[...]], [v_ref[...], vc_ref[...]], [None, None]))
    for sb, o in enumerate(_gqa_core(blocks, sink_ref)):
        o_ref[sb * tq:(sb + 1) * tq, :] = o.astype(BF16)


def _window_band_table(tq):
    off = jnp.arange(3)[:, None, None] * WA_WINDOW
    qpos = (jnp.arange(2 * tq) % tq)[None, :, None] + off
    kpos = jnp.arange(tq + 2 * WA_WINDOW)[None, None, :]
    return jnp.where(jnp.abs(qpos - kpos) <= WA_WINDOW, 0.0, NEG_INF).astype(F32)


def _gqa_attention(qkv, sink, nlat, window):
    nb, seq, _ = qkv.shape
    tq, sub_blocks = (WINDOW_TQ, WINDOW_SUB_BLOCKS) if window else (DENSE_TQ, DENSE_SUB_BLOCKS)
    step = tq * sub_blocks
    in_specs = [
        pl.BlockSpec((None, step, 256), lambda b, i: (b, i, 0)),
        pl.BlockSpec((None, seq, LANE), lambda b, i: (b, 0, 2)),
        pl.BlockSpec((None, seq, LANE), lambda b, i: (b, 0, 3)),
        pl.BlockSpec((None, CTX_LEN, LANE), lambda b, i: (nlat, b, 2)),
        pl.BlockSpec((None, CTX_LEN, LANE), lambda b, i: (nlat, b, 3)),
    ]
    args = [qkv, qkv, qkv, qkv, qkv]
    if window:
        band = _window_band_table(tq)
        in_specs = [pl.BlockSpec(memory_space=pltpu.SMEM), _const_spec(band.shape)] + in_specs
        args = [sink, band] + args
    return pl.pallas_call(
        functools.partial(_gqa_kernel, window=window, seq=seq, tq=tq, sub_blocks=sub_blocks),
        out_shape=jax.ShapeDtypeStruct((nlat, seq, BRANCH_W), BF16),
        grid=(nlat, seq // step),
        in_specs=in_specs,
        out_specs=pl.BlockSpec((None, step, BRANCH_W), lambda b, i: (b, i, 0)),
        compiler_params=_cparams(("arbitrary", "arbitrary")),
        name="window_attention" if window else "dense_attention",
    )(*args)


NA_STEP_ROWS = 4
NA_WIN_ROWS = NA_STEP_ROWS + NA_ROWS
NA_TQ = NA_STEP_ROWS * GRID_W
NA_TK = NA_WIN_ROWS * GRID_W


def _na_window_start(group, rows):
    return jnp.clip(group * NA_STEP_ROWS - NA_ROWS // 2, 0, rows - NA_WIN_ROWS)


NA_STEP_GROUPS = 4


def _na_kernel(q_ref, k_ref, v_ref, kc_ref, vc_ref, bias_ref, o_ref, *, rows):
    lo = _lane_iota((1, LANE)) < HEAD_DIM
    chains = [(gi, blk) for gi in range(NA_STEP_GROUPS) for blk in range(2)]
    starts = [pl.multiple_of(_na_window_start(pl.program_id(0) * NA_STEP_GROUPS + gi, rows) * GRID_W, GRID_W)
              for gi in range(NA_STEP_GROUPS)]
    scores = []
    for gi, blk in chains:
        cols = slice(blk * LANE, (blk + 1) * LANE)
        qb = q_ref[gi * NA_TQ:(gi + 1) * NA_TQ, cols]
        zero = jnp.zeros_like(qb)
        qs = jnp.concatenate([jnp.where(lo, qb, zero), jnp.where(lo, zero, qb)], axis=0)
        s_nb = _dot_nt(qs, k_ref[pl.ds(starts[gi], NA_TK), cols]) + bias_ref[gi, blk]
        scores.append([s_nb, _dot_nt(qs, kc_ref[:, cols])])
    outs = []
    for (gi, blk), sc in zip(chains, scores):
        cols = slice(blk * LANE, (blk + 1) * LANE)
        o = _softmax_pv(sc, [v_ref[pl.ds(starts[gi], NA_TK), cols], vc_ref[:, cols]])
        outs.append(jnp.where(lo, o[0:NA_TQ], o[NA_TQ:2 * NA_TQ]))
    for gi in range(NA_STEP_GROUPS):
        o_ref[gi * NA_TQ:(gi + 1) * NA_TQ, :] = jnp.concatenate(outs[2 * gi:2 * gi + 2], axis=1).astype(BF16)


def _na_attention(qkv, bias, layer, nlat):
    nb, seq, _ = qkv.shape
    rows = seq // GRID_W
    tq = NA_STEP_GROUPS * NA_TQ
    return pl.pallas_call(
        functools.partial(_na_kernel, rows=rows),
        out_shape=jax.ShapeDtypeStruct((nlat, seq, BRANCH_W), BF16),
        grid=(seq // tq, nlat),
        in_specs=[
            pl.BlockSpec((None, tq, 256), lambda g, b: (b, g, 0)),
            pl.BlockSpec((None, seq, 256), lambda g, b: (b, 0, 1)),
            pl.BlockSpec((None, seq, 256), lambda g, b: (b, 0, 2)),
            pl.BlockSpec((None, CTX_LEN, 256), lambda g, b: (nlat, b, 1)),
            pl.BlockSpec((None, CTX_LEN, 256), lambda g, b: (nlat, b, 2)),
            pl.BlockSpec((None, NA_STEP_GROUPS, 2, 2 * NA_TQ, NA_TK), lambda g, b: (layer, g, 0, 0, 0)),
        ],
        out_specs=pl.BlockSpec((None, tq, BRANCH_W), lambda g, b: (b, g, 0)),
        compiler_params=_cparams(("arbitrary", "arbitrary")),
        name="neighborhood_attention",
    )(qkv, qkv, qkv, qkv, qkv, bias)


def _ctx_kernel(sink_ref, qa_ref, ka_ref, va_ref, qb_ref, kb_ref, vb_ref, qc_ref, kc_ref, vc_ref,
                oa_ref, ob_ref, oc_ref):
    t = CTX_LEN
    oa_ref[...] = _gqa_core([(qa_ref[...], [ka_ref[...]], [va_ref[...]], [None])], sink_ref)[0].astype(BF16)
    oc_ref[...] = _gqa_core([(qc_ref[...], [kc_ref[...]], [vc_ref[...]], [None])], None)[0].astype(BF16)
    lo = _lane_iota((1, LANE)) < HEAD_DIM
    outs = []
    for blk in range(2):
        cols = slice(blk * LANE, (blk + 1) * LANE)
        qb = qb_ref[:, cols]
        zero = jnp.zeros_like(qb)
        qs = jnp.concatenate([jnp.where(lo, qb, zero), jnp.where(lo, zero, qb)], axis=0)
        o = _softmax_pv([_dot_nt(qs, kb_ref[:, cols])], [vb_ref[:, cols]])
        outs.append(jnp.where(lo, o[0:t], o[t:2 * t]))
    ob_ref[...] = jnp.concatenate(outs, axis=1).astype(BF16)


def _ctx_attention(sink, qa, qb, qc, nlat):
    blk = lambda w, j: pl.BlockSpec((None, CTX_LEN, w), lambda b: (nlat, b, j))
    out_spec = pl.BlockSpec((None, CTX_LEN, BRANCH_W), lambda b: (0, b, 0))
    return pl.pallas_call(
        _ctx_kernel,
        out_shape=[jax.ShapeDtypeStruct((1, nlat * CTX_LEN, BRANCH_W), BF16)] * 3,
        grid=(nlat,),
        in_specs=[
            pl.BlockSpec(memory_space=pltpu.SMEM),
            blk(256, 0), blk(LANE, 2), blk(LANE, 3),
            blk(256, 0), blk(256, 1), blk(256, 2),
            blk(256, 0), blk(LANE, 2), blk(LANE, 3),
        ],
        out_specs=[out_spec] * 3,
        compiler_params=_cparams(("arbitrary",)),
        name="context_attention",
    )(sink, qa, qa, qa, qb, qb, qb, qc, qc, qc)


CONV_PAD = 8
FIN_ROWS = 256


def _split3(t):
    hi = t.astype(BF16)
    r1 = t - hi.astype(F32)
    mid = r1.astype(BF16)
    return hi, mid, (r1 - mid.astype(F32)).astype(BF16)


LOG2_E = 1.4426950408889634
HEAD_ROWS = 16
CONV_PHASES = 4
N_SLAB = SSM_CONV_CH // LANE


def _ssd_kernel(xbc_ref, xbcc_ref, dt_ref, dtc_ref, z_ref, zc_ref, cw_ref, cb_ref, dtb_ref, alog_ref,
                dvec_ref, nrm_ref, y_ref, yc_ref, pad_ref, u_ref, ct_ref, dt8_ref, w_ref, e_ref,
                yacc_ref, loc_ref, ce_ref, *, seq):
    ltot = CTX_LEN + seq
    nchunk = ltot // SSM_CHUNK
    nctx = CTX_LEN // SSM_CHUNK
    half = SSM_CONV // 2
    cs = SSM_CHUNK
    hr = HEAD_ROWS

    def conv_into(src_ref, n, dst):
        for sl in range(N_SLAB):
            pad_ref[sl, 0:CONV_PAD, :] = jnp.zeros((CONV_PAD, LANE), F32)
            pad_ref[sl, CONV_PAD + n:2 * CONV_PAD + n, :] = jnp.zeros((CONV_PAD, LANE), F32)
            pad_ref[sl, CONV_PAD:CONV_PAD + n, :] = src_ref[:, sl * LANE:(sl + 1) * LANE]
        q = FIN_ROWS // CONV_PHASES

        def block(i, carry):
            r0 = pl.multiple_of(i * FIN_ROWS, FIN_ROWS)
            for sl in range(N_SLAB):
                lanes = slice(sl * LANE, (sl + 1) * LANE)
                for p in range(CONV_PHASES):
                    acc = jnp.broadcast_to(cb_ref[:, lanes], (q, LANE))
                    for k in range(SSM_CONV):
                        tap = pad_ref[sl, pl.ds(r0 + CONV_PAD + p + k - half, q, stride=CONV_PHASES), :]
                        acc = acc + tap * cw_ref[k:k + 1, lanes]
                    u_ref[sl, pl.ds(dst + r0 + p, q, stride=CONV_PHASES), :] = _silu(acc)
            return carry

        lax.fori_loop(0, n // FIN_ROWS, block, 0)

    conv_into(xbcc_ref, CTX_LEN, 0)
    conv_into(xbc_ref, seq, CTX_LEN)

    row = lax.broadcasted_iota(jnp.int32, (cs, cs), 0)
    col = lax.broadcasted_iota(jnp.int32, (cs, cs), 1)
    causal = (col <= row, col >= row)
    lo = _lane_iota((1, LANE)) < SSM_STATE // 2

    def softplus(t):
        return jnp.maximum(t, 0.0) + jnp.log1p(jnp.exp(-jnp.abs(t)))

    sel = jnp.where(lax.broadcasted_iota(jnp.int32, (hr, LANE), 0) == lax.broadcasted_iota(jnp.int32, (hr, LANE), 1),
                    1.0, 0.0).astype(BF16)

    def heads_to_rows(src_ref):
        return sum(_dot_nt(sel, part) for part in _split3(src_ref[...]))

    dt_t = softplus(jnp.concatenate([heads_to_rows(dtc_ref), heads_to_rows(dt_ref)], axis=1) + dtb_ref[...])
    da_t = dt_t * (-jnp.exp(alog_ref[...]))

    def by_chunk(t):
        return jnp.concatenate([t[:, c * cs:(c + 1) * cs] for c in range(nchunk)], axis=0)

    dt_c, da_c = by_chunk(dt_t), by_chunk(da_t)
    tri = jnp.where(row <= col, 1.0, 0.0).astype(BF16)
    pre = sum(_dot(part, tri) for part in _split3(da_c))
    is_fwd = lax.broadcasted_iota(jnp.int32, (nchunk * hr, 1), 0) % hr < SSM_HEADS
    ct = jnp.where(is_fwd, pre, (pre[:, cs - 1:cs] - pre) + da_c) * LOG2_E
    tot = jnp.where(is_fwd, ct[:, cs - 1:cs], ct[:, 0:1])
    ct_ref[...] = ct
    dt8_ref[...] = dt_c
    w_ref[...] = dt_c * jnp.exp2(tot - ct)
    e_ref[...] = jnp.broadcast_to(jnp.exp2(tot), (nchunk * hr, cs))

    def chunk_rows(c):
        return pl.ds(pl.multiple_of(c * cs, cs), cs)

    def head_rows(c):
        return pl.ds(pl.multiple_of(c * hr, hr), hr)

    def phase1(c, carry):
        rows = chunk_rows(c)
        ct8 = ct_ref[head_rows(c), :]
        dt8 = dt8_ref[head_rows(c), :]
        w8 = w_ref[head_rows(c), :]
        for g in range(SSM_GROUPS):
            gl = slice(g * LANE, (g + 1) * LANE)
            xg = u_ref[g, rows, :]
            bg = u_ref[SSM_GROUPS + g, rows, :]
            cg = u_ref[2 * SSM_GROUPS + g, rows, :]
            xcat = jnp.concatenate([jnp.where(lo, xg, 0.0), jnp.where(lo, 0.0, xg)], axis=0).astype(BF16)
            bt = bg.T
            cb = _dot_nt(cg.astype(BF16), bg.astype(BF16))
            mh = [None, None]
            ces, bts = [], []
            for d in range(2):
                for hh in range(2):
                    j = d * SSM_HEADS + 2 * g + hh
                    cum_j = jnp.broadcast_to(ct8[j:j + 1, :], (cs, cs)).T
                    seg = jnp.where(causal[d], cum_j - ct8[j:j + 1, :], NEG_INF)
                    term = jnp.exp2(seg) * dt8[j:j + 1, :]
                    mh[hh] = term if d == 0 else mh[hh] + term
                    ces.append((cg * jnp.exp2(cum_j)).astype(BF16))
                    bts.append((bt * w8[j:j + 1, :]).astype(BF16))
            m = jnp.concatenate([(cb * mh[0]).astype(BF16), (cb * mh[1]).astype(BF16)], axis=1)
            yacc_ref[rows, gl] = _dot(m, xcat)
            ce_ref[rows, g * 4 * LANE:(g + 1) * 4 * LANE] = jnp.concatenate(ces, axis=1)
            for d in range(2):
                st = _dot(jnp.concatenate([bts[2 * d], bts[2 * d + 1]], axis=1), xcat)
                loc_ref[c, :, d * SSM_INNER + g * LANE:d * SSM_INNER + (g + 1) * LANE] = st
        return carry

    lax.fori_loop(0, nchunk, phase1, 0, unroll=2)

    def scan(order, d):
        cols = slice(d * SSM_INNER, (d + 1) * SSM_INNER)

        def body(t, s):
            c = order(t)
            local = loc_ref[c, :, cols]
            loc_ref[c, :, cols] = s
            e8 = e_ref[head_rows(c), :]
            j0 = d * SSM_HEADS
            decay = jnp.concatenate([jnp.where(lo, e8[j0 + 2 * g:j0 + 2 * g + 1, :], e8[j0 + 2 * g + 1:j0 + 2 * g + 2, :])
                                     for g in range(SSM_GROUPS)], axis=1)
            return s * decay + local
        lax.fori_loop(0, nchunk, body, jnp.zeros((SSM_STATE, SSM_INNER), F32))

    scan(lambda t: t, 0)
    scan(lambda t: jnp.where(t < nctx, nctx - 1 - t, nchunk + nctx - 1 - t), 1)

    def finish_chunk(c, z_rows, out_ref, out_rows):
        rows = chunk_rows(c)
        s_in = loc_ref[c]
        ys = []
        for g in range(SSM_GROUPS):
            gl = slice(g * LANE, (g + 1) * LANE)
            sf = s_in[:, gl]
            sb = s_in[:, SSM_INNER + g * LANE:SSM_INNER + (g + 1) * LANE]
            scat = jnp.concatenate([jnp.where(lo, sf, 0.0), jnp.where(lo, 0.0, sf),
                                    jnp.where(lo, sb, 0.0), jnp.where(lo, 0.0, sb)], axis=0).astype(BF16)
            ys.append(yacc_ref[rows, gl] + _dot(ce_ref[rows, g * 4 * LANE:(g + 1) * 4 * LANE], scat))
        xs = jnp.concatenate([u_ref[g, rows, :] for g in range(SSM_GROUPS)], axis=1)
        y = jnp.concatenate(ys, axis=1) + dvec_ref[...] * xs
        y = y * _silu(z_rows)
        out_ref[out_rows, :] = _rms(y, nrm_ref[...]).astype(BF16)

    for c in range(nctx):
        finish_chunk(c, zc_ref[c * cs:(c + 1) * cs, :], yc_ref, slice(c * cs, (c + 1) * cs))

    def finish_body(cl, carry):
        out_rows = chunk_rows(cl)
        finish_chunk(cl + nctx, z_ref[out_rows, :], y_ref, out_rows)
        return carry

    lax.fori_loop(0, nchunk - nctx, finish_body, 0, unroll=2)


def _ssd(xbc, dt, z, conv_w, conv_b, dt_bias, a_log, dvec, nrm, nlat):
    nb, seq, _ = xbc.shape
    ltot = CTX_LEN + seq
    nchunk = ltot // SSM_CHUNK
    lat = lambda w: pl.BlockSpec((None, seq, w), lambda b: (b, 0, 0))
    ctx = lambda w: pl.BlockSpec((None, CTX_LEN, w), lambda b: (nlat, b, 0))
    return pl.pallas_call(
        functools.partial(_ssd_kernel, seq=seq),
        out_shape=[jax.ShapeDtypeStruct((nlat, seq, SSM_INNER), BF16),
                   jax.ShapeDtypeStruct((nlat, CTX_LEN, SSM_INNER), BF16)],
        grid=(nlat,),
        in_specs=[
            lat(SSM_CONV_CH), ctx(SSM_CONV_CH), lat(DT_PAD), ctx(DT_PAD), lat(SSM_INNER), ctx(SSM_INNER),
            _const_spec((8, SSM_CONV_CH)), _const_spec((1, SSM_CONV_CH)),
            _const_spec((HEAD_ROWS, 1)), _const_spec((HEAD_ROWS, 1)),
            _const_spec((1, SSM_INNER)), _const_spec((1, SSM_INNER)),
        ],
        out_specs=[pl.BlockSpec((None, seq, SSM_INNER), lambda b: (b, 0, 0)),
                   pl.BlockSpec((None, CTX_LEN, SSM_INNER), lambda b: (b, 0, 0))],
        scratch_shapes=[
            pltpu.VMEM((N_SLAB, seq + 2 * CONV_PAD, LANE), F32),
            pltpu.VMEM((N_SLAB, ltot, LANE), F32),
            pltpu.VMEM((nchunk * HEAD_ROWS, SSM_CHUNK), F32),
            pltpu.VMEM((nchunk * HEAD_ROWS, SSM_CHUNK), F32),
            pltpu.VMEM((nchunk * HEAD_ROWS, SSM_CHUNK), F32),
            pltpu.VMEM((nchunk * HEAD_ROWS, SSM_CHUNK), F32),
            pltpu.VMEM((ltot, SSM_INNER), F32),
            pltpu.VMEM((nchunk, SSM_STATE, 2 * SSM_INNER), F32),
            pltpu.VMEM((ltot, 2 * SSM_GROUPS * 2 * SSM_STATE), BF16),
        ],
        compiler_params=_cparams(("arbitrary",)),
        name="ssd",
    )(xbc, xbc, dt, dt, z, z, conv_w, conv_b, dt_bias, a_log, dvec, nrm)


MERGE_TM = 512


def _merge_kernel(*refs, nlat_split):
    x_ref, sh_ref, sc_ref, gt_ref, g_ref = refs[:5]
    y_refs = refs[5:9]
    yc_refs = refs[9:13] if nlat_split is not None else None
    wgl_ref, bg_ref, wbr_ref, wo_ref, o_ref = refs[-5:]
    x = x_ref[...]
    xn = _ada(x, g_ref[...], sh_ref[...], sc_ref[...]).astype(BF16)
    mix = None
    for k in range(N_BRANCH):
        cols = slice(k * D_MODEL, (k + 1) * D_MODEL)
        pre = _dot(xn, wgl_ref[:, cols]) + bg_ref[:, cols]
        gate = 1.0 / (1.0 + jnp.exp(-pre))
        y = y_refs[k][...]
        if nlat_split is not None:
            y = jnp.where(pl.program_id(0) == nlat_split, yc_refs[k][...], y)
        term = gate * _dot(y, wbr_ref[k])
        mix = term if mix is None else mix + term
    o_ref[...] = x + gt_ref[...] * _dot(mix.astype(BF16), wo_ref[...])


def _merge(x_all, mod_l, gain, ys, ys_ctx, w_gate, b_gate, w_branch, w_out, layer, nb):
    seq = x_all.shape[1]
    nlat = ys[0].shape[0]
    tok = lambda w: pl.BlockSpec((None, MERGE_TM, w), lambda b, t: (b, t, 0))
    lat = pl.BlockSpec((None, MERGE_TM, BRANCH_W), lambda b, t: (jnp.minimum(b, nlat - 1), t, 0))
    in_specs = [tok(D_MODEL), _mod_spec(3), _mod_spec(4), _mod_spec(5), _const_spec((1, D_MODEL))] + [lat] * 4
    args = [x_all, mod_l, mod_l, mod_l, gain.reshape(1, D_MODEL), *ys]
    if ys_ctx is not None:
        in_specs += [pl.BlockSpec((None, MERGE_TM, BRANCH_W), lambda b, t: (0, t, 0))] * 4
        args += list(ys_ctx)
    in_specs += [
        _layer_spec((D_MODEL, N_BRANCH * D_MODEL), layer), _layer_spec((1, N_BRANCH * D_MODEL), layer),
        _layer_spec((N_BRANCH, BRANCH_W, D_MODEL), layer), _layer_spec((D_MODEL, D_MODEL), layer),
    ]
    args += [w_gate, b_gate.reshape(DEPTH, 1, N_BRANCH * D_MODEL), w_branch, w_out]
    return pl.pallas_call(
        functools.partial(_merge_kernel, nlat_split=nlat if ys_ctx is not None else None),
        out_shape=jax.ShapeDtypeStruct((nb, seq, D_MODEL), F32),
        grid=(nb, seq // MERGE_TM),
        in_specs=in_specs,
        out_specs=tok(D_MODEL),
        compiler_params=_cparams(("arbitrary", "arbitrary")),
        name="merge",
    )(*args)


def _rope_table(seq):
    pos = jnp.arange(seq)
    axes = jnp.stack([pos // GRID_W, pos % GRID_W], axis=-1).astype(F32)
    quarter = HEAD_DIM // 4
    inv = 1.0 / (ROPE_THETA ** (jnp.arange(quarter, dtype=F32) * 4.0 / HEAD_DIM))
    ang = axes[:, :, None] * inv
    cos, sin = jnp.cos(ang), jnp.sin(ang)
    cos_h = jnp.concatenate([cos, cos], axis=-1).reshape(seq, HEAD_DIM)
    sin_h = jnp.concatenate([-sin, sin], axis=-1).reshape(seq, HEAD_DIM)
    lat = jnp.concatenate([jnp.tile(cos_h, (1, 4)), jnp.tile(sin_h, (1, 4))], axis=-1)
    ident = jnp.concatenate([jnp.ones((seq, 256), F32), jnp.zeros((seq, 256), F32)], axis=-1)
    return jnp.stack([lat, ident])


def _permute_heads(w, axis):
    parts = jnp.split(w, 4, axis=axis)
    return jnp.concatenate([parts[j] for j in Q_HEAD_ORDER], axis=axis)


CAST_SPLIT = 4


def _cast_kernel(w_ref, o_ref):
    o_ref[...] = w_ref[...].astype(BF16)


def _cast_bf16(w):
    depth, r, c = w.shape
    tr = r // CAST_SPLIT
    spec = pl.BlockSpec((None, tr, c), lambda l, i: (l, i, 0))
    return pl.pallas_call(
        _cast_kernel,
        out_shape=jax.ShapeDtypeStruct(w.shape, BF16),
        grid=(depth, CAST_SPLIT),
        in_specs=[spec], out_specs=spec,
        compiler_params=_cparams(("arbitrary", "arbitrary")),
        name="cast_bf16",
    )(w)


MXU_TILE = 256
GATE_COL0 = PDT + 2 * SSM_HEADS


def _transpose_cast_kernel(w_ref, o_ref):
    eye = jnp.where(lax.broadcasted_iota(jnp.int32, (MXU_TILE, MXU_TILE), 0)
                    == lax.broadcasted_iota(jnp.int32, (MXU_TILE, MXU_TILE), 1), 1.0, 0.0).astype(BF16)
    for k0 in range(0, D_MODEL, MXU_TILE):
        blk = w_ref[:, k0:k0 + MXU_TILE].astype(BF16)
        o_ref[k0:k0 + MXU_TILE, :] = _dot_nt(eye, blk).astype(BF16)


def _pack_w_in(w_in_t):
    depth = w_in_t.shape[0]
    proj = pl.pallas_call(
        _transpose_cast_kernel,
        out_shape=jax.ShapeDtypeStruct((depth, D_MODEL, PROJ_N), BF16),
        grid=(depth,),
        in_specs=[pl.BlockSpec((None, PROJ_N, D_MODEL), lambda l: (l, 0, 0))],
        out_specs=pl.BlockSpec((None, D_MODEL, PROJ_N), lambda l: (l, 0, 0)),
        compiler_params=_cparams(("arbitrary",)),
        name="pack_proj",
    )(w_in_t)
    gate = pl.pallas_call(
        lambda w_ref, o_ref: _transpose_cast_kernel(w_ref.at[0], o_ref),
        out_shape=jax.ShapeDtypeStruct((depth, D_MODEL, N_BRANCH * D_MODEL), BF16),
        grid=(depth, N_BRANCH),
        in_specs=[pl.BlockSpec((pl.Element(1), pl.Element(D_MODEL), pl.Element(D_MODEL)),
                               lambda l, k: (l, pl.multiple_of(GATE_COL0 + k * D_MODEL, 8), 0))],
        out_specs=pl.BlockSpec((None, D_MODEL, D_MODEL), lambda l, k: (l, 0, k)),
        compiler_params=_cparams(("arbitrary", "arbitrary")),
        name="pack_gate",
    )(w_in_t)
    return proj, gate


NA_BANDS = 2 * NA_ROWS - 1


def _na_bias_kernel(rpb_ref, o_ref, band_ref, *, rows):
    group = pl.program_id(1)
    c = lax.broadcasted_iota(jnp.int32, (GRID_W, LANE), 0)
    kc = lax.broadcasted_iota(jnp.int32, (GRID_W, LANE), 1)
    lo = kc < GRID_W

    @pl.when(group == 0)
    def _():
        qstart = jnp.clip(c - NA_COLS // 2, 0, GRID_W - NA_COLS)
        ok = (kc >= qstart) & (kc < qstart + NA_COLS)
        for h in range(2 * 2):
            for dr in range(NA_BANDS):
                line = jnp.broadcast_to(rpb_ref[h, dr:dr + 1, :], (GRID_W, LANE))
                skew = pltpu.roll(line, LANE - (NA_COLS - 1), 1, stride=1, stride_axis=0)
                band_ref[h, dr] = jnp.where(ok, skew, NEG_INF)

    w0 = _na_window_start(group, rows)
    for h in range(2 * 2):
        blk, g = divmod(h, 2)
        for qi in range(NA_STEP_ROWS):
            r = group * NA_STEP_ROWS + qi
            rs = jnp.clip(r - NA_ROWS // 2, 0, rows - NA_ROWS)
            for m in range(NA_WIN_ROWS // 2):
                halves = []
                for kr in (w0 + 2 * m, w0 + 2 * m + 1):
                    in_window = (kr >= rs) & (kr < rs + NA_ROWS)
                    dr = jnp.clip(kr - r + NA_ROWS - 1, 0, NA_BANDS - 1)
                    halves.append(jnp.where(in_window, band_ref[h, dr], NEG_INF))
                pair = jnp.where(lo, halves[0], pltpu.roll(halves[1], GRID_W, 1))
                o_ref[blk, g * NA_TQ + qi * GRID_W:g * NA_TQ + (qi + 1) * GRID_W, m * LANE:(m + 1) * LANE] = pair


def _na_bias_table(rpb, rows):
    depth, heads = rpb.shape[:2]
    groups = rows // NA_STEP_ROWS
    padded = jnp.pad(rpb, ((0, 0), (0, 0), (0, 16 - rpb.shape[2]), (0, LANE - rpb.shape[3])))
    return pl.pallas_call(
        functools.partial(_na_bias_kernel, rows=rows),
        out_shape=jax.ShapeDtypeStruct((depth, groups, 2, 2 * NA_TQ, NA_TK), F32),
        grid=(depth, groups),
        in_specs=[pl.BlockSpec((None, heads, 16, LANE), lambda l, g: (l, 0, 0, 0))],
        out_specs=pl.BlockSpec((None, None, 2, 2 * NA_TQ, NA_TK), lambda l, g: (l, g, 0, 0, 0)),
        scratch_shapes=[pltpu.VMEM((heads, 16, GRID_W, LANE), F32)],
        compiler_params=_cparams(("arbitrary", "arbitrary")),
        name="na_bias",
    )(padded)


def kernel(x, c, ctx, c_ctx, w_mod, b_mod, norm_ffn1, ffn1_w_gate, ffn1_w_up, ffn1_w_down, norm_mix, w_in, b_gate,
           attn_sink, na_rpb, qk_norm_q, qk_norm_k, ssm_conv_w, ssm_conv_b, ssm_dt_bias, ssm_a_log, ssm_d, ssm_norm,
           w_branch, w_out, norm_ffn2, ffn2_w_gate, ffn2_w_up, ffn2_w_down, final_norm):
    nlat, seq, _ = x.shape
    assert nlat * CTX_LEN == seq and nlat + 1 <= MOD_ROWS
    nb = nlat + 1
    c_rows = jnp.concatenate([c, c_ctx[None], jnp.zeros((MOD_ROWS - nb, D_MODEL), F32)], axis=0)
    mod = _modulation(c_rows, w_mod, b_mod).reshape(DEPTH, N_MOD, MOD_ROWS, 1, D_MODEL)
    rope_tab = _rope_table(seq)
    head_col = lambda v: jnp.pad(v.reshape(-1, 1), ((0, HEAD_ROWS - v.size), (0, 0)))

    ffn1_w = (ffn1_w_gate, ffn1_w_up, ffn1_w_down)
    ffn2_w = (ffn2_w_gate, ffn2_w_up, ffn2_w_down)
    w_proj, w_gl = _pack_w_in(jnp.swapaxes(w_in, 1, 2))
    w_o = _cast_bf16(w_out)
    wbr = jnp.stack([_permute_heads(w_branch[:, 0], 1), w_branch[:, 1],
                     _permute_heads(w_branch[:, 2], 1), w_branch[:, 3]], axis=1).astype(BF16)
    na_bias = _na_bias_table(na_rpb, seq // GRID_W)

    for l in range(DEPTH):
        with_ctx = l < DEPTH - 1
        last = l == DEPTH - 1
        nq = jnp.tile(qk_norm_q[l], 4).reshape(1, 256)
        nk = jnp.tile(qk_norm_k[l], 4).reshape(1, 256)
        conv_w = jnp.pad(ssm_conv_w[l], ((0, 8 - SSM_CONV), (0, 0)))
        dvec = jnp.repeat(ssm_d[l], SSM_INNER // SSM_HEADS).reshape(1, SSM_INNER)

        if l == 0:
            x_all = _ffn(x, mod[l], (0, 1, 2), norm_ffn1[l], *ffn1_w, l, nb, x_ctx=ctx.reshape(1, seq, D_MODEL))
        else:
            x_all = _ffn(x_all, mod[l], (0, 1, 2), norm_ffn1[l], *ffn1_w, l, nb)
        qa, qb, qc, z, xbc, dt = _proj(x_all, mod[l], norm_mix[l], w_proj, l, rope_tab, nq, nk, nlat)
        ya = _gqa_attention(qa, attn_sink[l], nlat, window=True)
        yb = _na_attention(qb, na_bias, l, nlat)
        yc = _gqa_attention(qc, None, nlat, window=False)
        yd, yd_ctx = _ssd(xbc, dt, z, conv_w, ssm_conv_b[l].reshape(1, -1), head_col(ssm_dt_bias[l]),
                          head_col(ssm_a_log[l]), dvec, ssm_norm[l].reshape(1, -1), nlat)
        nbm = nb if with_ctx else nlat
        ys_ctx = None
        if with_ctx:
            ys_ctx = (*_ctx_attention(attn_sink[l], qa, qb, qc, nlat), yd_ctx.reshape(1, seq, SSM_INNER))
        x_all = _merge(x_all, mod[l], norm_mix[l], (ya, yb, yc, yd), ys_ctx, w_gl, b_gate, wbr, w_o, l, nbm)
        x_all = _ffn(x_all, mod[l], (6, 7, 8), norm_ffn2[l], *ffn2_w, l, nbm,
                     final_gain=final_norm if last else None)
    return x_all
```

```python
import functools

import jax
import jax.numpy as jnp
from jax import lax
from jax.experimental import pallas as pl
from jax.experimental.pallas import tpu as pltpu

D_MODEL = 1024
DEPTH = 4
CTX_LEN = 256
GRID_W = 64
HEAD_DIM = 64
ROPE_THETA = 10000.0
EPS = 1e-6
NEG_INF = -1e30
D_FF = 2816
N_BRANCH = 4
BRANCH_W = 256
WA_WINDOW = 128
NA_ROWS = 8
NA_COLS = 16
SSM_HEADS = 4
SSM_GROUPS = 2
SSM_STATE = 128
SSM_CONV = 5
SSM_CHUNK = 128
SSM_INNER = 256
SSM_CONV_CH = 768
N_MOD = 9
MOD_ROWS = 16
LANE = 128
DT_PAD = LANE
QK_SCALE = HEAD_DIM ** -0.5
VMEM_LIMIT = 56 * 1024 * 1024

BF16 = jnp.bfloat16
F32 = jnp.float32


def _cparams(sem):
    return pltpu.CompilerParams(dimension_semantics=sem, vmem_limit_bytes=VMEM_LIMIT)


def _const_spec(shape):
    nd = len(shape)
    return pl.BlockSpec(shape, lambda *_: (0,) * nd)


def _layer_spec(shape, layer):
    nd = len(shape)
    return pl.BlockSpec((None,) + tuple(shape), lambda *_: (layer,) + (0,) * nd)


def _dot(a, b):
    return jnp.dot(a, b, preferred_element_type=F32)


def _dot_nt(a, b):
    return lax.dot_general(a, b, (((1,), (1,)), ((), ())), preferred_element_type=F32)


def _silu(t):
    return t * (1.0 / (1.0 + jnp.exp(-t)))


def _rms(x, g):
    return (x * lax.rsqrt(jnp.mean(x * x, axis=-1, keepdims=True) + EPS)) * g


def _ada(x, g, shift, scale):
    return _rms(x, g) * (1.0 + scale) + shift


def _mod_kernel(c_ref, w_ref, b_ref, o_ref):
    act = _silu(c_ref[...]).astype(BF16)
    o_ref[...] = _dot(act, w_ref[...].astype(BF16)) + b_ref[...]


def _modulation(c_rows, w_mod, b_mod):
    return pl.pallas_call(
        _mod_kernel,
        out_shape=jax.ShapeDtypeStruct((DEPTH, N_MOD, MOD_ROWS, D_MODEL), F32),
        grid=(DEPTH, N_MOD),
        in_specs=[
            pl.BlockSpec((MOD_ROWS, D_MODEL), lambda l, i: (0, 0)),
            pl.BlockSpec((None, D_MODEL, D_MODEL), lambda l, i: (l, 0, i)),
            pl.BlockSpec((None, 1, D_MODEL), lambda l, i: (l, 0, i)),
        ],
        out_specs=pl.BlockSpec((None, None, MOD_ROWS, D_MODEL), lambda l, i: (l, i, 0, 0)),
        compiler_params=_cparams(("arbitrary", "arbitrary")),
        name="modulation",
    )(c_rows, w_mod, b_mod.reshape(DEPTH, 1, N_MOD * D_MODEL))


def _mod_spec(kind):
    return pl.BlockSpec((None, None, 1, D_MODEL), lambda b, t: (kind, b, 0, 0))


FFN_TM = 512
FFN_CHUNKS = ((0, 1024), (1024, 2048), (2048, 2816))


FFN_STAGE_ROWS = 256


FFN_STAGE_SLOTS = 4


def _stream_cast(src, dst, stage, sem):
    rows = src.shape[0]
    nblk = rows // FFN_STAGE_ROWS
    ahead = FFN_STAGE_SLOTS - 1

    def copy(i):
        slot = i % FFN_STAGE_SLOTS
        return pltpu.make_async_copy(src.at[pl.ds(i * FFN_STAGE_ROWS, FFN_STAGE_ROWS), :], stage.at[slot],
                                     sem.at[slot])

    for i in range(min(ahead, nblk)):
        copy(i).start()
    for i in range(nblk):
        if i + ahead < nblk:
            copy(i + ahead).start()
        copy(i).wait()
        dst[i * FFN_STAGE_ROWS:(i + 1) * FFN_STAGE_ROWS, :] = stage[i % FFN_STAGE_SLOTS].astype(BF16)


def _ffn_kernel(*refs, final, nlat_split, layer):
    refs = list(refs)
    x_ref = refs.pop(0)
    xc_ref = refs.pop(0) if nlat_split is not None else None
    sh_ref, sc_ref, gt_ref, g_ref, wg_hbm, wu_hbm, wd_hbm = refs[:7]
    fg_ref = refs[7] if final else None
    o_ref, wg_ref, wu_ref, wd_ref, stage_in, stage_out, sem = refs[-7:]

    @pl.when((pl.program_id(0) == 0) & (pl.program_id(1) == 0))
    def _():
        _stream_cast(wg_hbm.at[layer], wg_ref, stage_in, sem)
        _stream_cast(wu_hbm.at[layer], wu_ref, stage_in, sem)
        _stream_cast(wd_hbm.at[layer], wd_ref, stage_out, sem)

    x = x_ref[...]
    if nlat_split is not None:
        x = jnp.where(pl.program_id(0) == nlat_split, xc_ref[...], x)
    xn = _ada(x, g_ref[...], sh_ref[...], sc_ref[...]).astype(BF16)
    acc = None
    for c0, c1 in FFN_CHUNKS:
        gate = _dot(xn, wg_ref[:, c0:c1])
        up = _dot(xn, wu_ref[:, c0:c1])
        h = (_silu(gate) * up).astype(BF16)
        part = _dot(h, wd_ref[c0:c1, :])
        acc = part if acc is None else acc + part
    y = x + (0.5 * gt_ref[...]) * acc
    if final:
        y = _rms(y, fg_ref[...])
    o_ref[...] = y


def _ffn(x_all, mod_l, kinds, gain, wg, wu, wd, layer, nb, final_gain=None, x_ctx=None):
    seq = x_all.shape[1]
    hbm = pl.BlockSpec(memory_space=pl.ANY)
    final = final_gain is not None
    tok = pl.BlockSpec((None, FFN_TM, D_MODEL), lambda b, t: (b, t, 0))
    if x_ctx is None:
        nlat_split = None
        in_specs, args = [tok], [x_all]
    else:
        nlat_split = x_all.shape[0]
        in_specs = [pl.BlockSpec((None, FFN_TM, D_MODEL), lambda b, t: (jnp.minimum(b, nlat_split - 1), t, 0)),
                    pl.BlockSpec((None, FFN_TM, D_MODEL), lambda b, t: (0, t, 0))]
        args = [x_all, x_ctx]
    in_specs += [
        _mod_spec(kinds[0]), _mod_spec(kinds[1]), _mod_spec(kinds[2]),
        _const_spec((1, D_MODEL)),
        hbm, hbm, hbm,
    ]
    args += [mod_l, mod_l, mod_l, gain.reshape(1, D_MODEL), wg, wu, wd]
    if final:
        in_specs.append(_const_spec((1, D_MODEL)))
        args.append(final_gain.reshape(1, D_MODEL))
    return pl.pallas_call(
        functools.partial(_ffn_kernel, final=final, nlat_split=nlat_split, layer=layer),
        out_shape=jax.ShapeDtypeStruct((nb, seq, D_MODEL), F32),
        grid=(nb, seq // FFN_TM),
        in_specs=in_specs,
        out_specs=pl.BlockSpec((None, FFN_TM, D_MODEL), lambda b, t: (b, t, 0)),
        scratch_shapes=[
            pltpu.VMEM((D_MODEL, D_FF), BF16), pltpu.VMEM((D_MODEL, D_FF), BF16), pltpu.VMEM((D_FF, D_MODEL), BF16),
            pltpu.VMEM((FFN_STAGE_SLOTS, FFN_STAGE_ROWS, D_FF), F32),
            pltpu.VMEM((FFN_STAGE_SLOTS, FFN_STAGE_ROWS, D_MODEL), F32),
            pltpu.SemaphoreType.DMA((FFN_STAGE_SLOTS,)),
        ],
        compiler_params=_cparams(("arbitrary", "arbitrary")),
        name="ffn",
    )(*args)


PA, PB, PC, PZ, PX, PDT = 0, 512, 1280, 1792, 2048, 2816
PROJ_N = PDT + DT_PAD
PROJ_TM = 512
Q_HEAD_ORDER = (0, 2, 1, 3)


def _lane_iota(shape):
    return lax.broadcasted_iota(jnp.int32, shape, len(shape) - 1)


def _rope(t, cos, sin):
    w = t.shape[-1]
    first = (_lane_iota((1, w)) % 32) < 16
    rot = jnp.where(first, pltpu.roll(t, w - 16, 1), pltpu.roll(t, 16, 1))
    return t * cos[:, :w] + rot * sin[:, :w]


def _head_norm(t, gain):
    w = t.shape[-1]
    lane = _lane_iota((1, w))
    sq = t * t
    scale = jnp.zeros_like(t)
    for h in range(w // HEAD_DIM):
        m = (lane >= h * HEAD_DIM) & (lane < (h + 1) * HEAD_DIM)
        ms = jnp.sum(jnp.where(m, sq, 0.0), axis=-1, keepdims=True) * (1.0 / HEAD_DIM)
        scale = jnp.where(m, lax.rsqrt(ms + EPS), scale)
    return (t * scale) * gain[:, :w]


def _pair_heads(t):
    lo = _lane_iota((1, LANE)) < HEAD_DIM
    t0, t1 = t[:, 0:LANE], t[:, LANE:2 * LANE]
    return jnp.concatenate([jnp.where(lo, t0, pltpu.roll(t1, HEAD_DIM, 1)),
                            jnp.where(lo, pltpu.roll(t0, HEAD_DIM, 1), t1)], axis=1)


def _proj_kernel(x_ref, sh_ref, sc_ref, g_ref, w_ref, rope_ref, nq_ref, nk_ref,
                 qa_ref, qb_ref, qc_ref, z_ref, xbc_ref, dt_ref):
    xn = _ada(x_ref[...], g_ref[...], sh_ref[...], sc_ref[...]).astype(BF16)
    cos = rope_ref[:, 0:256]
    sin = rope_ref[:, 256:512]
    a = _dot(xn, w_ref[:, PA:PA + 512])
    qa_ref[:, 0:256] = _pair_heads(_rope(a[:, 0:256], cos, sin) * QK_SCALE).astype(BF16)
    qa_ref[:, 256:384] = _rope(a[:, 256:384], cos, sin).astype(BF16)
    qa_ref[:, 384:512] = a[:, 384:512].astype(BF16)
    b = _dot(xn, w_ref[:, PB:PB + 768])
    qb_ref[:, 0:256] = (b[:, 0:256] * QK_SCALE).astype(BF16)
    qb_ref[:, 256:768] = b[:, 256:768].astype(BF16)
    c = _dot(xn, w_ref[:, PC:PC + 512])
    qc_ref[:, 0:256] = _pair_heads(_rope(_head_norm(c[:, 0:256], nq_ref[...]), cos, sin) * QK_SCALE).astype(BF16)
    qc_ref[:, 256:384] = _rope(_head_norm(c[:, 256:384], nk_ref[...]), cos, sin).astype(BF16)
    qc_ref[:, 384:512] = c[:, 384:512].astype(BF16)
    z_ref[...] = _dot(xn, w_ref[:, PZ:PZ + 256])
    xbc_ref[...] = _dot(xn, w_ref[:, PX:PX + 768])
    dt_ref[...] = _dot(xn, w_ref[:, PDT:PDT + DT_PAD])


def _proj(x_all, mod_l, gain, w_proj, layer, rope_tab, nq, nk, nlat):
    nb, seq, _ = x_all.shape
    tok = lambda w: pl.BlockSpec((None, PROJ_TM, w), lambda b, t: (b, t, 0))
    outs = [(512, BF16), (768, BF16), (512, BF16), (256, F32), (768, F32), (DT_PAD, F32)]
    return pl.pallas_call(
        _proj_kernel,
        out_shape=[jax.ShapeDtypeStruct((nb, seq, w), dt) for w, dt in outs],
        grid=(nb, seq // PROJ_TM),
        in_specs=[
            tok(D_MODEL), _mod_spec(3), _mod_spec(4),
            _const_spec((1, D_MODEL)),
            _layer_spec((D_MODEL, PROJ_N), layer),
            pl.BlockSpec((None, PROJ_TM, 512), lambda b, t: (b // nlat, t, 0)),
            _const_spec((1, 256)), _const_spec((1, 256)),
        ],
        out_specs=[tok(w) for w, _ in outs],
        compiler_params=_cparams(("arbitrary", "arbitrary")),
        name="in_proj",
    )(x_all, mod_l, mod_l, gain.reshape(1, D_MODEL), w_proj, rope_tab, nq, nk)


def _gqa_core(blocks, sink_ref):
    lo = _lane_iota((1, LANE)) < HEAD_DIM
    chain_scores = []
    for q, keys, _, biases in blocks:
        t0, t1 = q[:, 0:LANE], q[:, LANE:2 * LANE]
        zero = jnp.zeros_like(t0)
        for hk in range(2):
            if hk == 0:
                qs = jnp.concatenate([jnp.where(lo, t0, zero), jnp.where(lo, t1, zero)], axis=0)
            else:
                qs = jnp.concatenate([jnp.where(lo, zero, t0), jnp.where(lo, zero, t1)], axis=0)
            scores = []
            for k, bias in zip(keys, biases):
                s = _dot_nt(qs, k)
                scores.append(s if bias is None else s + bias)
            chain_scores.append(scores)
    results = []
    for sb, (q, _, values, _) in enumerate(blocks):
        t = q.shape[0]
        outs = []
        for hk in range(2):
            extra = None
            if sink_ref is not None:
                row = lax.broadcasted_iota(jnp.int32, (2 * t, 1), 0)
                extra = jnp.where(row < t, sink_ref[2 * hk], sink_ref[2 * hk + 1])
            outs.append(_softmax_pv(chain_scores[2 * sb + hk], values, extra, spare_lanes=~lo if hk == 0 else lo))
        o0, o1 = outs
        results.append(jnp.concatenate([jnp.where(lo, o0[0:t], o1[0:t]),
                                        jnp.where(lo, o0[t:2 * t], o1[t:2 * t])], axis=1))
    return results


def _softmax_pv(scores, values, extra_logit=None, spare_lanes=None):
    m = None
    for s in scores:
        sm = jnp.max(s, axis=-1, keepdims=True)
        m = sm if m is None else jnp.maximum(m, sm)
    if extra_logit is not None:
        m = jnp.maximum(m, extra_logit)
    acc = None
    for s, v in zip(scores, values):
        p = jnp.exp((s - m).astype(BF16))
        if spare_lanes is None:
            pv = _dot(p, jnp.concatenate([v, jnp.ones_like(v)], axis=1))
        else:
            pv = _dot(p, jnp.where(spare_lanes, jnp.ones_like(v), v))
        acc = pv if acc is None else acc + pv
    if spare_lanes is None:
        den = acc[:, LANE:2 * LANE]
        acc = acc[:, 0:LANE]
    else:
        den = pltpu.roll(acc, HEAD_DIM, 1)
    if extra_logit is not None:
        den = den + jnp.exp(extra_logit - m)
    return acc * (1.0 / den)


WINDOW_TQ = 256
WINDOW_SUB_BLOCKS = 4
DENSE_TQ = 128
DENSE_SUB_BLOCKS = 4


def _gqa_kernel(*refs, window, seq, tq, sub_blocks):
    if window:
        sink_ref, band_ref, q_ref, k_ref, v_ref, kc_ref, vc_ref, o_ref = refs
    else:
        q_ref, k_ref, v_ref, kc_ref, vc_ref, o_ref = refs
        sink_ref = None
    blocks = []
    for sb in range(sub_blocks):
        q = q_ref[sb * tq:(sb + 1) * tq, :]
        if window:
            nkw = tq + 2 * WA_WINDOW
            q0 = (pl.program_id(1) * sub_blocks + sb) * tq
            start = pl.multiple_of(jnp.clip(q0 - WA_WINDOW, 0, seq - nkw), LANE)
            bias = band_ref[(q0 - start) // WA_WINDOW]
            blocks.append((q, [k_ref[pl.ds(start, nkw), :], kc_ref[...]], [v_ref[pl.ds(start, nkw), :], vc_ref[...]],
                           [bias, None]))
        else:
            blocks.append((q, [k_ref[...], kc_ref[...]], [v_ref[...], vc_ref[...]], [None, None]))
    for sb, o in enumerate(_gqa_core(blocks, sink_ref)):
        o_ref[sb * tq:(sb + 1) * tq, :] = o.astype(BF16)


def _window_band_table(tq):
    off = jnp.arange(3)[:, None, None] * WA_WINDOW
    qpos = (jnp.arange(2 * tq) % tq)[None, :, None] + off
    kpos = jnp.arange(tq + 2 * WA_WINDOW)[None, None, :]
    return jnp.where(jnp.abs(qpos - kpos) <= WA_WINDOW, 0.0, NEG_INF).astype(F32)


def _gqa_attention(qkv, sink, nlat, window):
    nb, seq, _ = qkv.shape
    tq, sub_blocks = (WINDOW_TQ, WINDOW_SUB_BLOCKS) if window else (DENSE_TQ, DENSE_SUB_BLOCKS)
    step = tq * sub_blocks
    in_specs = [
        pl.BlockSpec((None, step, 256), lambda b, i: (b, i, 0)),
        pl.BlockSpec((None, seq, LANE), lambda b, i: (b, 0, 2)),
        pl.BlockSpec((None, seq, LANE), lambda b, i: (b, 0, 3)),
        pl.BlockSpec((None, CTX_LEN, LANE), lambda b, i: (nlat, b, 2)),
        pl.BlockSpec((None, CTX_LEN, LANE), lambda b, i: (nlat, b, 3)),
    ]
    args = [qkv, qkv, qkv, qkv, qkv]
    if window:
        band = _window_band_table(tq)
        in_specs = [pl.BlockSpec(memory_space=pltpu.SMEM), _const_spec(band.shape)] + in_specs
        args = [sink, band] + args
    return pl.pallas_call(
        functools.partial(_gqa_kernel, window=window, seq=seq, tq=tq, sub_blocks=sub_blocks),
        out_shape=jax.ShapeDtypeStruct((nlat, seq, BRANCH_W), BF16),
        grid=(nlat, seq // step),
        in_specs=in_specs,
        out_specs=pl.BlockSpec((None, step, BRANCH_W), lambda b, i: (b, i, 0)),
        compiler_params=_cparams(("arbitrary", "arbitrary")),
        name="window_attention" if window else "dense_attention",
    )(*args)


NA_STEP_ROWS = 4
NA_WIN_ROWS = NA_STEP_ROWS + NA_ROWS
NA_TQ = NA_STEP_ROWS * GRID_W
NA_TK = NA_WIN_ROWS * GRID_W


def _na_window_start(group, rows):
    return jnp.clip(group * NA_STEP_ROWS - NA_ROWS // 2, 0, rows - NA_WIN_ROWS)


NA_STEP_GROUPS = 4


def _na_kernel(q_ref, k_ref, v_ref, kc_ref, vc_ref, bias_ref, o_ref, *, rows):
    lo = _lane_iota((1, LANE)) < HEAD_DIM
    chains = [(gi, blk) for gi in range(NA_STEP_GROUPS) for blk in range(2)]
    starts = [pl.multiple_of(_na_window_start(pl.program_id(0) * NA_STEP_GROUPS + gi, rows) * GRID_W, GRID_W)
              for gi in range(NA_STEP_GROUPS)]
    scores = []
    for gi, blk in chains:
        cols = slice(blk * LANE, (blk + 1) * LANE)
        qb = q_ref[gi * NA_TQ:(gi + 1) * NA_TQ, cols]
        zero = jnp.zeros_like(qb)
        qs = jnp.concatenate([jnp.where(lo, qb, zero), jnp.where(lo, zero, qb)], axis=0)
        s_nb = _dot_nt(qs, k_ref[pl.ds(starts[gi], NA_TK), cols]) + bias_ref[gi, blk]
        scores.append([s_nb, _dot_nt(qs, kc_ref[:, cols])])
    outs = []
    for (gi, blk), sc in zip(chains, scores):
        cols = slice(blk * LANE, (blk + 1) * LANE)
        o = _softmax_pv(sc, [v_ref[pl.ds(starts[gi], NA_TK), cols], vc_ref[:, cols]])
        outs.append(jnp.where(lo, o[0:NA_TQ], o[NA_TQ:2 * NA_TQ]))
    for gi in range(NA_STEP_GROUPS):
        o_ref[gi * NA_TQ:(gi + 1) * NA_TQ, :] = jnp.concatenate(outs[2 * gi:2 * gi + 2], axis=1).astype(BF16)


def _na_attention(qkv, bias, layer, nlat):
    nb, seq, _ = qkv.shape
    rows = seq // GRID_W
    tq = NA_STEP_GROUPS * NA_TQ
    return pl.pallas_call(
        functools.partial(_na_kernel, rows=rows),
        out_shape=jax.ShapeDtypeStruct((nlat, seq, BRANCH_W), BF16),
        grid=(seq // tq, nlat),
        in_specs=[
            pl.BlockSpec((None, tq, 256), lambda g, b: (b, g, 0)),
            pl.BlockSpec((None, seq, 256), lambda g, b: (b, 0, 1)),
            pl.BlockSpec((None, seq, 256), lambda g, b: (b, 0, 2)),
            pl.BlockSpec((None, CTX_LEN, 256), lambda g, b: (nlat, b, 1)),
            pl.BlockSpec((None, CTX_LEN, 256), lambda g, b: (nlat, b, 2)),
            pl.BlockSpec((None, NA_STEP_GROUPS, 2, 2 * NA_TQ, NA_TK), lambda g, b: (layer, g, 0, 0, 0)),
        ],
        out_specs=pl.BlockSpec((None, tq, BRANCH_W), lambda g, b: (b, g, 0)),
        compiler_params=_cparams(("arbitrary", "arbitrary")),
        name="neighborhood_attention",
    )(qkv, qkv, qkv, qkv, qkv, bias)


def _ctx_kernel(sink_ref, qa_ref, ka_ref, va_ref, qb_ref, kb_ref, vb_ref, qc_ref, kc_ref, vc_ref,
                oa_ref, ob_ref, oc_ref):
    t = CTX_LEN
    oa_ref[...] = _gqa_core([(qa_ref[...], [ka_ref[...]], [va_ref[...]], [None])], sink_ref)[0].astype(BF16)
    oc_ref[...] = _gqa_core([(qc_ref[...], [kc_ref[...]], [vc_ref[...]], [None])], None)[0].astype(BF16)
    lo = _lane_iota((1, LANE)) < HEAD_DIM
    outs = []
    for blk in range(2):
        cols = slice(blk * LANE, (blk + 1) * LANE)
        qb = qb_ref[:, cols]
        zero = jnp.zeros_like(qb)
        qs = jnp.concatenate([jnp.where(lo, qb, zero), jnp.where(lo, zero, qb)], axis=0)
        o = _softmax_pv([_dot_nt(qs, kb_ref[:, cols])], [vb_ref[:, cols]])
        outs.append(jnp.where(lo, o[0:t], o[t:2 * t]))
    ob_ref[...] = jnp.concatenate(outs, axis=1).astype(BF16)


def _ctx_attention(sink, qa, qb, qc, nlat):
    blk = lambda w, j: pl.BlockSpec((None, CTX_LEN, w), lambda b: (nlat, b, j))
    out_spec = pl.BlockSpec((None, CTX_LEN, BRANCH_W), lambda b: (0, b, 0))
    return pl.pallas_call(
        _ctx_kernel,
        out_shape=[jax.ShapeDtypeStruct((1, nlat * CTX_LEN, BRANCH_W), BF16)] * 3,
        grid=(nlat,),
        in_specs=[
            pl.BlockSpec(memory_space=pltpu.SMEM),
            blk(256, 0), blk(LANE, 2), blk(LANE, 3),
            blk(256, 0), blk(256, 1), blk(256, 2),
            blk(256, 0), blk(LANE, 2), blk(LANE, 3),
        ],
        out_specs=[out_spec] * 3,
        compiler_params=_cparams(("arbitrary",)),
        name="context_attention",
    )(sink, qa, qa, qa, qb, qb, qb, qc, qc, qc)


CONV_PAD = 8
FIN_ROWS = 256


def _split3(t):
    hi = t.astype(BF16)
    r1 = t - hi.astype(F32)
    mid = r1.astype(BF16)
    return hi, mid, (r1 - mid.astype(F32)).astype(BF16)


LOG2_E = 1.4426950408889634
HEAD_ROWS = 16
CONV_PHASES = 4
N_SLAB = SSM_CONV_CH // LANE


def _ssd_kernel(xbc_ref, xbcc_ref, dt_ref, dtc_ref, z_ref, zc_ref, cw_ref, cb_ref, dtb_ref, alog_ref,
                dvec_ref, nrm_ref, y_ref, yc_ref, pad_ref, u_ref, ct_ref, dt8_ref, w_ref, e_ref,
                yacc_ref, loc_ref, ce_ref, *, seq):
    ltot = CTX_LEN + seq
    nchunk = ltot // SSM_CHUNK
    nctx = CTX_LEN // SSM_CHUNK
    half = SSM_CONV // 2
    cs = SSM_CHUNK
    hr = HEAD_ROWS

    def conv_into(src_ref, n, dst):
        for sl in range(N_SLAB):
            pad_ref[sl, 0:CONV_PAD, :] = jnp.zeros((CONV_PAD, LANE), F32)
            pad_ref[sl, CONV_PAD + n:2 * CONV_PAD + n, :] = jnp.zeros((CONV_PAD, LANE), F32)
            pad_ref[sl, CONV_PAD:CONV_PAD + n, :] = src_ref[:, sl * LANE:(sl + 1) * LANE]
        q = FIN_ROWS // CONV_PHASES

        def block(i, carry):
            r0 = pl.multiple_of(i * FIN_ROWS, FIN_ROWS)
            for sl in range(N_SLAB):
                lanes = slice(sl * LANE, (sl + 1) * LANE)
                for p in range(CONV_PHASES):
                    acc = jnp.broadcast_to(cb_ref[:, lanes], (q, LANE))
                    for k in range(SSM_CONV):
                        tap = pad_ref[sl, pl.ds(r0 + CONV_PAD + p + k - half, q, stride=CONV_PHASES), :]
                        acc = acc + tap * cw_ref[k:k + 1, lanes]
                    u_ref[sl, pl.ds(dst + r0 + p, q, stride=CONV_PHASES), :] = _silu(acc)
            return carry

        lax.fori_loop(0, n // FIN_ROWS, block, 0)

    conv_into(xbcc_ref, CTX_LEN, 0)
    conv_into(xbc_ref, seq, CTX_LEN)

    row = lax.broadcasted_iota(jnp.int32, (cs, cs), 0)
    col = lax.broadcasted_iota(jnp.int32, (cs, cs), 1)
    causal = (col <= row, col >= row)
    lo = _lane_iota((1, LANE)) < SSM_STATE // 2

    def softplus(t):
        return jnp.maximum(t, 0.0) + jnp.log1p(jnp.exp(-jnp.abs(t)))

    sel = jnp.where(lax.broadcasted_iota(jnp.int32, (hr, LANE), 0) == lax.broadcasted_iota(jnp.int32, (hr, LANE), 1),
                    1.0, 0.0).astype(BF16)

    def heads_to_rows(src_ref):
        return sum(_dot_nt(sel, part) for part in _split3(src_ref[...]))

    dt_t = softplus(jnp.concatenate([heads_to_rows(dtc_ref), heads_to_rows(dt_ref)], axis=1) + dtb_ref[...])
    da_t = dt_t * (-jnp.exp(alog_ref[...]))

    def by_chunk(t):
        return jnp.concatenate([t[:, c * cs:(c + 1) * cs] for c in range(nchunk)], axis=0)

    dt_c, da_c = by_chunk(dt_t), by_chunk(da_t)
    tri = jnp.where(row <= col, 1.0, 0.0).astype(BF16)
    pre = sum(_dot(part, tri) for part in _split3(da_c))
    is_fwd = lax.broadcasted_iota(jnp.int32, (nchunk * hr, 1), 0) % hr < SSM_HEADS
    ct = jnp.where(is_fwd, pre, (pre[:, cs - 1:cs] - pre) + da_c) * LOG2_E
    tot = jnp.where(is_fwd, ct[:, cs - 1:cs], ct[:, 0:1])
    ct_ref[...] = ct
    dt8_ref[...] = dt_c
    w_ref[...] = dt_c * jnp.exp2(tot - ct)
    e_ref[...] = jnp.broadcast_to(jnp.exp2(tot), (nchunk * hr, cs))

    def chunk_rows(c):
        return pl.ds(pl.multiple_of(c * cs, cs), cs)

    def head_rows(c):
        return pl.ds(pl.multiple_of(c * hr, hr), hr)

    def phase1(c, carry):
        rows = chunk_rows(c)
        ct8 = ct_ref[head_rows(c), :]
        dt8 = dt8_ref[head_rows(c), :]
        w8 = w_ref[head_rows(c), :]
        for g in range(SSM_GROUPS):
            gl = slice(g * LANE, (g + 1) * LANE)
            xg = u_ref[g, rows, :]
            bg = u_ref[SSM_GROUPS + g, rows, :]
            cg = u_ref[2 * SSM_GROUPS + g, rows, :]
            xcat = jnp.concatenate([jnp.where(lo, xg, 0.0), jnp.where(lo, 0.0, xg)], axis=0).astype(BF16)
            bt = bg.T
            cb = _dot_nt(cg.astype(BF16), bg.astype(BF16))
            mh = [None, None]
            ces, bts = [], []
            for d in range(2):
                for hh in range(2):
                    j = d * SSM_HEADS + 2 * g + hh
                    cum_j = jnp.broadcast_to(ct8[j:j + 1, :], (cs, cs)).T
                    seg = jnp.where(causal[d], cum_j - ct8[j:j + 1, :], NEG_INF)
                    term = jnp.exp2(seg) * dt8[j:j + 1, :]
                    mh[hh] = term if d == 0 else mh[hh] + term
                    ces.append((cg * jnp.exp2(cum_j)).astype(BF16))
                    bts.append((bt * w8[j:j + 1, :]).astype(BF16))
            m = jnp.concatenate([(cb * mh[0]).astype(BF16), (cb * mh[1]).astype(BF16)], axis=1)
            yacc_ref[rows, gl] = _dot(m, xcat)
            ce_ref[rows, g * 4 * LANE:(g + 1) * 4 * LANE] = jnp.concatenate(ces, axis=1)
            for d in range(2):
                st = _dot(jnp.concatenate([bts[2 * d], bts[2 * d + 1]], axis=1), xcat)
                loc_ref[c, :, d * SSM_INNER + g * LANE:d * SSM_INNER + (g + 1) * LANE] = st
        return carry

    lax.fori_loop(0, nchunk, phase1, 0, unroll=2)

    def scan(order, d):
        cols = slice(d * SSM_INNER, (d + 1) * SSM_INNER)

        def body(t, s):
            c = order(t)
            local = loc_ref[c, :, cols]
            loc_ref[c, :, cols] = s
            e8 = e_ref[head_rows(c), :]
            j0 = d * SSM_HEADS
            decay = jnp.concatenate([jnp.where(lo, e8[j0 + 2 * g:j0 + 2 * g + 1, :], e8[j0 + 2 * g + 1:j0 + 2 * g + 2, :])
                                     for g in range(SSM_GROUPS)], axis=1)
            return s * decay + local
        lax.fori_loop(0, nchunk, body, jnp.zeros((SSM_STATE, SSM_INNER), F32))

    scan(lambda t: t, 0)
    scan(lambda t: jnp.where(t < nctx, nctx - 1 - t, nchunk + nctx - 1 - t), 1)

    def finish_chunk(c, z_rows, out_ref, out_rows):
        rows = chunk_rows(c)
        s_in = loc_ref[c]
        ys = []
        for g in range(SSM_GROUPS):
            gl = slice(g * LANE, (g + 1) * LANE)
            sf = s_in[:, gl]
            sb = s_in[:, SSM_INNER + g * LANE:SSM_INNER + (g + 1) * LANE]
            scat = jnp.concatenate([jnp.where(lo, sf, 0.0), jnp.where(lo, 0.0, sf),
                                    jnp.where(lo, sb, 0.0), jnp.where(lo, 0.0, sb)], axis=0).astype(BF16)
            ys.append(yacc_ref[rows, gl] + _dot(ce_ref[rows, g * 4 * LANE:(g + 1) * 4 * LANE], scat))
        xs = jnp.concatenate([u_ref[g, rows, :] for g in range(SSM_GROUPS)], axis=1)
        y = jnp.concatenate(ys, axis=1) + dvec_ref[...] * xs
        y = y * _silu(z_rows)
        out_ref[out_rows, :] = _rms(y, nrm_ref[...]).astype(BF16)

    for c in range(nctx):
        finish_chunk(c, zc_ref[c * cs:(c + 1) * cs, :], yc_ref, slice(c * cs, (c + 1) * cs))

    def finish_body(cl, carry):
        out_rows = chunk_rows(cl)
        finish_chunk(cl + nctx, z_ref[out_rows, :], y_ref, out_rows)
        return carry

    lax.fori_loop(0, nchunk - nctx, finish_body, 0, unroll=2)


def _ssd(xbc, dt, z, conv_w, conv_b, dt_bias, a_log, dvec, nrm, nlat):
    nb, seq, _ = xbc.shape
    ltot = CTX_LEN + seq
    nchunk = ltot // SSM_CHUNK
    lat = lambda w: pl.BlockSpec((None, seq, w), lambda b: (b, 0, 0))
    ctx = lambda w: pl.BlockSpec((None, CTX_LEN, w), lambda b: (nlat, b, 0))
    return pl.pallas_call(
        functools.partial(_ssd_kernel, seq=seq),
        out_shape=[jax.ShapeDtypeStruct((nlat, seq, SSM_INNER), BF16),
                   jax.ShapeDtypeStruct((nlat, CTX_LEN, SSM_INNER), BF16)],
        grid=(nlat,),
        in_specs=[
            lat(SSM_CONV_CH), ctx(SSM_CONV_CH), lat(DT_PAD), ctx(DT_PAD), lat(SSM_INNER), ctx(SSM_INNER),
            _const_spec((8, SSM_CONV_CH)), _const_spec((1, SSM_CONV_CH)),
            _const_spec((HEAD_ROWS, 1)), _const_spec((HEAD_ROWS, 1)),
            _const_spec((1, SSM_INNER)), _const_spec((1, SSM_INNER)),
        ],
        out_specs=[pl.BlockSpec((None, seq, SSM_INNER), lambda b: (b, 0, 0)),
                   pl.BlockSpec((None, CTX_LEN, SSM_INNER), lambda b: (b, 0, 0))],
        scratch_shapes=[
            pltpu.VMEM((N_SLAB, seq + 2 * CONV_PAD, LANE), F32),
            pltpu.VMEM((N_SLAB, ltot, LANE), F32),
            pltpu.VMEM((nchunk * HEAD_ROWS, SSM_CHUNK), F32),
            pltpu.VMEM((nchunk * HEAD_ROWS, SSM_CHUNK), F32),
            pltpu.VMEM((nchunk * HEAD_ROWS, SSM_CHUNK), F32),
            pltpu.VMEM((nchunk * HEAD_ROWS, SSM_CHUNK), F32),
            pltpu.VMEM((ltot, SSM_INNER), F32),
            pltpu.VMEM((nchunk, SSM_STATE, 2 * SSM_INNER), F32),
            pltpu.VMEM((ltot, 2 * SSM_GROUPS * 2 * SSM_STATE), BF16),
        ],
        compiler_params=_cparams(("arbitrary",)),
        name="ssd",
    )(xbc, xbc, dt, dt, z, z, conv_w, conv_b, dt_bias, a_log, dvec, nrm)


MERGE_TM = 512


def _merge_kernel(*refs, nlat_split):
    x_ref, sh_ref, sc_ref, gt_ref, g_ref = refs[:5]
    y_refs = refs[5:9]
    yc_refs = refs[9:13] if nlat_split is not None else None
    wgl_ref, bg_ref, wbr_ref, wo_ref, o_ref = refs[-5:]
    x = x_ref[...]
    xn = _ada(x, g_ref[...], sh_ref[...], sc_ref[...]).astype(BF16)
    mix = None
    for k in range(N_BRANCH):
        cols = slice(k * D_MODEL, (k + 1) * D_MODEL)
        pre = _dot(xn, wgl_ref[:, cols]) + bg_ref[:, cols]
        gate = 1.0 / (1.0 + jnp.exp(-pre))
        y = y_refs[k][...]
        if nlat_split is not None:
            y = jnp.where(pl.program_id(0) == nlat_split, yc_refs[k][...], y)
        term = gate * _dot(y, wbr_ref[k])
        mix = term if mix is None else mix + term
    o_ref[...] = x + gt_ref[...] * _dot(mix.astype(BF16), wo_ref[...])


def _merge(x_all, mod_l, gain, ys, ys_ctx, w_gate, b_gate, w_branch, w_out, layer, nb):
    seq = x_all.shape[1]
    nlat = ys[0].shape[0]
    tok = lambda w: pl.BlockSpec((None, MERGE_TM, w), lambda b, t: (b, t, 0))
    lat = pl.BlockSpec((None, MERGE_TM, BRANCH_W), lambda b, t: (jnp.minimum(b, nlat - 1), t, 0))
    in_specs = [tok(D_MODEL), _mod_spec(3), _mod_spec(4), _mod_spec(5), _const_spec((1, D_MODEL))] + [lat] * 4
    args = [x_all, mod_l, mod_l, mod_l, gain.reshape(1, D_MODEL), *ys]
    if ys_ctx is not None:
        in_specs += [pl.BlockSpec((None, MERGE_TM, BRANCH_W), lambda b, t: (0, t, 0))] * 4
        args += list(ys_ctx)
    in_specs += [
        _layer_spec((D_MODEL, N_BRANCH * D_MODEL), layer), _layer_spec((1, N_BRANCH * D_MODEL), layer),
        _layer_spec((N_BRANCH, BRANCH_W, D_MODEL), layer), _layer_spec((D_MODEL, D_MODEL), layer),
    ]
    args += [w_gate, b_gate.reshape(DEPTH, 1, N_BRANCH * D_MODEL), w_branch, w_out]
    return pl.pallas_call(
        functools.partial(_merge_kernel, nlat_split=nlat if ys_ctx is not None else None),
        out_shape=jax.ShapeDtypeStruct((nb, seq, D_MODEL), F32),
        grid=(nb, seq // MERGE_TM),
        in_specs=in_specs,
        out_specs=tok(D_MODEL),
        compiler_params=_cparams(("arbitrary", "arbitrary")),
        name="merge",
    )(*args)


def _rope_table(seq):
    pos = jnp.arange(seq)
    axes = jnp.stack([pos // GRID_W, pos % GRID_W], axis=-1).astype(F32)
    quarter = HEAD_DIM // 4
    inv = 1.0 / (ROPE_THETA ** (jnp.arange(quarter, dtype=F32) * 4.0 / HEAD_DIM))
    ang = axes[:, :, None] * inv
    cos, sin = jnp.cos(ang), jnp.sin(ang)
    cos_h = jnp.concatenate([cos, cos], axis=-1).reshape(seq, HEAD_DIM)
    sin_h = jnp.concatenate([-sin, sin], axis=-1).reshape(seq, HEAD_DIM)
    lat = jnp.concatenate([jnp.tile(cos_h, (1, 4)), jnp.tile(sin_h, (1, 4))], axis=-1)
    ident = jnp.concatenate([jnp.ones((seq, 256), F32), jnp.zeros((seq, 256), F32)], axis=-1)
    return jnp.stack([lat, ident])


def _permute_heads(w, axis):
    parts = jnp.split(w, 4, axis=axis)
    return jnp.concatenate([parts[j] for j in Q_HEAD_ORDER], axis=axis)


CAST_SPLIT = 4


def _cast_kernel(w_ref, o_ref):
    o_ref[...] = w_ref[...].astype(BF16)


def _cast_bf16(w):
    depth, r, c = w.shape
    tr = r // CAST_SPLIT
    spec = pl.BlockSpec((None, tr, c), lambda l, i: (l, i, 0))
    return pl.pallas_call(
        _cast_kernel,
        out_shape=jax.ShapeDtypeStruct(w.shape, BF16),
        grid=(depth, CAST_SPLIT),
        in_specs=[spec], out_specs=spec,
        compiler_params=_cparams(("arbitrary", "arbitrary")),
        name="cast_bf16",
    )(w)


MXU_TILE = 256
GATE_COL0 = PDT + 2 * SSM_HEADS


def _transpose_cast_kernel(w_ref, o_ref):
    eye = jnp.where(lax.broadcasted_iota(jnp.int32, (MXU_TILE, MXU_TILE), 0)
                    == lax.broadcasted_iota(jnp.int32, (MXU_TILE, MXU_TILE), 1), 1.0, 0.0).astype(BF16)
    for k0 in range(0, D_MODEL, MXU_TILE):
        blk = w_ref[:, k0:k0 + MXU_TILE].astype(BF16)
        o_ref[k0:k0 + MXU_TILE, :] = _dot_nt(eye, blk).astype(BF16)


def _pack_w_in(w_in_t):
    depth = w_in_t.shape[0]
    proj = pl.pallas_call(
        _transpose_cast_kernel,
        out_shape=jax.ShapeDtypeStruct((depth, D_MODEL, PROJ_N), BF16),
        grid=(depth,),
        in_specs=[pl.BlockSpec((None, PROJ_N, D_MODEL), lambda l: (l, 0, 0))],
        out_specs=pl.BlockSpec((None, D_MODEL, PROJ_N), lambda l: (l, 0, 0)),
        compiler_params=_cparams(("arbitrary",)),
        name="pack_proj",
    )(w_in_t)
    gate = pl.pallas_call(
        lambda w_ref, o_ref: _transpose_cast_kernel(w_ref.at[0], o_ref),
        out_shape=jax.ShapeDtypeStruct((depth, D_MODEL, N_BRANCH * D_MODEL), BF16),
        grid=(depth, N_BRANCH),
        in_specs=[pl.BlockSpec((pl.Element(1), pl.Element(D_MODEL), pl.Element(D_MODEL)),
                               lambda l, k: (l, pl.multiple_of(GATE_COL0 + k * D_MODEL, 8), 0))],
        out_specs=pl.BlockSpec((None, D_MODEL, D_MODEL), lambda l, k: (l, 0, k)),
        compiler_params=_cparams(("arbitrary", "arbitrary")),
        name="pack_gate",
    )(w_in_t)
    return proj, gate


NA_BANDS = 2 * NA_ROWS - 1


def _na_bias_kernel(rpb_ref, o_ref, band_ref, *, rows):
    group = pl.program_id(1)
    c = lax.broadcasted_iota(jnp.int32, (GRID_W, LANE), 0)
    kc = lax.broadcasted_iota(jnp.int32, (GRID_W, LANE), 1)
    lo = kc < GRID_W

    @pl.when(group == 0)
    def _():
        qstart = jnp.clip(c - NA_COLS // 2, 0, GRID_W - NA_COLS)
        ok = (kc >= qstart) & (kc < qstart + NA_COLS)
        for h in range(2 * 2):
            for dr in range(NA_BANDS):
                line = jnp.broadcast_to(rpb_ref[h, dr:dr + 1, :], (GRID_W, LANE))
                skew = pltpu.roll(line, LANE - (NA_COLS - 1), 1, stride=1, stride_axis=0)
                band_ref[h, dr] = jnp.where(ok, skew, NEG_INF)

    w0 = _na_window_start(group, rows)
    for h in range(2 * 2):
        blk, g = divmod(h, 2)
        for qi in range(NA_STEP_ROWS):
            r = group * NA_STEP_ROWS + qi
            rs = jnp.clip(r - NA_ROWS // 2, 0, rows - NA_ROWS)
            for m in range(NA_WIN_ROWS // 2):
                halves = []
                for kr in (w0 + 2 * m, w0 + 2 * m + 1):
                    in_window = (kr >= rs) & (kr < rs + NA_ROWS)
                    dr = jnp.clip(kr - r + NA_ROWS - 1, 0, NA_BANDS - 1)
                    halves.append(jnp.where(in_window, band_ref[h, dr], NEG_INF))
                pair = jnp.where(lo, halves[0], pltpu.roll(halves[1], GRID_W, 1))
                o_ref[blk, g * NA_TQ + qi * GRID_W:g * NA_TQ + (qi + 1) * GRID_W, m * LANE:(m + 1) * LANE] = pair


def _na_bias_table(rpb, rows):
    depth, heads = rpb.shape[:2]
    groups = rows // NA_STEP_ROWS
    padded = jnp.pad(rpb, ((0, 0), (0, 0), (0, 16 - rpb.shape[2]), (0, LANE - rpb.shape[3])))
    return pl.pallas_call(
        functools.partial(_na_bias_kernel, rows=rows),
        out_shape=jax.ShapeDtypeStruct((depth, groups, 2, 2 * NA_TQ, NA_TK), F32),
        grid=(depth, groups),
        in_specs=[pl.BlockSpec((None, heads, 16, LANE), lambda l, g: (l, 0, 0, 0))],
        out_specs=pl.BlockSpec((None, None, 2, 2 * NA_TQ, NA_TK), lambda l, g: (l, g, 0, 0, 0)),
        scratch_shapes=[pltpu.VMEM((heads, 16, GRID_W, LANE), F32)],
        compiler_params=_cparams(("arbitrary", "arbitrary")),
        name="na_bias",
    )(padded)


def kernel(x, c, ctx, c_ctx, w_mod, b_mod, norm_ffn1, ffn1_w_gate, ffn1_w_up, ffn1_w_down, norm_mix, w_in, b_gate,
           attn_sink, na_rpb, qk_norm_q, qk_norm_k, ssm_conv_w, ssm_conv_b, ssm_dt_bias, ssm_a_log, ssm_d, ssm_norm,
           w_branch, w_out, norm_ffn2, ffn2_w_gate, ffn2_w_up, ffn2_w_down, final_norm):
    nlat, seq, _ = x.shape
    assert nlat * CTX_LEN == seq and nlat + 1 <= MOD_ROWS
    nb = nlat + 1
    c_rows = jnp.concatenate([c, c_ctx[None], jnp.zeros((MOD_ROWS - nb, D_MODEL), F32)], axis=0)
    mod = _modulation(c_rows, w_mod, b_mod).reshape(DEPTH, N_MOD, MOD_ROWS, 1, D_MODEL)
    rope_tab = _rope_table(seq)
    head_col = lambda v: jnp.pad(v.reshape(-1, 1), ((0, HEAD_ROWS - v.size), (0, 0)))

    ffn1_w = (ffn1_w_gate, ffn1_w_up, ffn1_w_down)
    ffn2_w = (ffn2_w_gate, ffn2_w_up, ffn2_w_down)
    w_proj, w_gl = _pack_w_in(jnp.swapaxes(w_in, 1, 2))
    w_o = _cast_bf16(w_out)
    wbr = jnp.stack([_permute_heads(w_branch[:, 0], 1), w_branch[:, 1],
                     _permute_heads(w_branch[:, 2], 1), w_branch[:, 3]], axis=1).astype(BF16)
    na_bias = _na_bias_table(na_rpb, seq // GRID_W)

    for l in range(DEPTH):
        with_ctx = l < DEPTH - 1
        last = l == DEPTH - 1
        nq = jnp.tile(qk_norm_q[l], 4).reshape(1, 256)
        nk = jnp.tile(qk_norm_k[l], 4).reshape(1, 256)
        conv_w = jnp.pad(ssm_conv_w[l], ((0, 8 - SSM_CONV), (0, 0)))
        dvec = jnp.repeat(ssm_d[l], SSM_INNER // SSM_HEADS).reshape(1, SSM_INNER)

        if l == 0:
            x_all = _ffn(x, mod[l], (0, 1, 2), norm_ffn1[l], *ffn1_w, l, nb, x_ctx=ctx.reshape(1, seq, D_MODEL))
        else:
            x_all = _ffn(x_all, mod[l], (0, 1, 2), norm_ffn1[l], *ffn1_w, l, nb)
        qa, qb, qc, z, xbc, dt = _proj(x_all, mod[l], norm_mix[l], w_proj, l, rope_tab, nq, nk, nlat)
        ya = _gqa_attention(qa, attn_sink[l], nlat, window=True)
        yb = _na_attention(qb, na_bias, l, nlat)
        yc = _gqa_attention(qc, None, nlat, window=False)
        yd, yd_ctx = _ssd(xbc, dt, z, conv_w, ssm_conv_b[l].reshape(1, -1), head_col(ssm_dt_bias[l]),
                          head_col(ssm_a_log[l]), dvec, ssm_norm[l].reshape(1, -1), nlat)
        nbm = nb if with_ctx else nlat
        ys_ctx = None
        if with_ctx:
            ys_ctx = (*_ctx_attention(attn_sink[l], qa, qb, qc, nlat), yd_ctx.reshape(1, seq, SSM_INNER))
        x_all = _merge(x_all, mod[l], norm_mix[l], (ya, yb, yc, yd), ys_ctx, w_gl, b_gate, wbr, w_o, l, nbm)
        x_all = _ffn(x_all, mod[l], (6, 7, 8), norm_ffn2[l], *ffn2_w, l, nbm,
                     final_gain=final_norm if last else None)
    return x_all
```

```python
import functools

import jax
import jax.numpy as jnp
from jax import lax
from jax.experimental import pallas as pl
from jax.experimental.pallas import tpu as pltpu

D_MODEL = 1024
DEPTH = 4
CTX_LEN = 256
GRID_W = 64
HEAD_DIM = 64
ROPE_THETA = 10000.0
EPS = 1e-6
NEG_INF = -1e30
D_FF = 2816
N_BRANCH = 4
BRANCH_W = 256
WA_WINDOW = 128
NA_ROWS = 8
NA_COLS = 16
SSM_HEADS = 4
SSM_GROUPS = 2
SSM_STATE = 128
SSM_CONV = 5
SSM_CHUNK = 128
SSM_INNER = 256
SSM_CONV_CH = 768
N_MOD = 9
MOD_ROWS = 16
LANE = 128
DT_PAD = LANE
QK_SCALE = HEAD_DIM ** -0.5
VMEM_LIMIT = 56 * 1024 * 1024

BF16 = jnp.bfloat16
F32 = jnp.float32


def _cparams(sem):
    return pltpu.CompilerParams(dimension_semantics=sem, vmem_limit_bytes=VMEM_LIMIT)


def _const_spec(shape):
    nd = len(shape)
    return pl.BlockSpec(shape, lambda *_: (0,) * nd)


def _layer_spec(shape, layer):
    nd = len(shape)
    return pl.BlockSpec((None,) + tuple(shape), lambda *_: (layer,) + (0,) * nd)


def _dot(a, b):
    return jnp.dot(a, b, preferred_element_type=F32)


def _dot_nt(a, b):
    return lax.dot_general(a, b, (((1,), (1,)), ((), ())), preferred_element_type=F32)


def _silu(t):
    return t * (1.0 / (1.0 + jnp.exp(-t)))


def _rms(x, g):
    return (x * lax.rsqrt(jnp.mean(x * x, axis=-1, keepdims=True) + EPS)) * g


def _ada(x, g, shift, scale):
    return _rms(x, g) * (1.0 + scale) + shift


def _mod_kernel(c_ref, w_ref, b_ref, o_ref):
    act = _silu(c_ref[...]).astype(BF16)
    o_ref[...] = _dot(act, w_ref[...].astype(BF16)) + b_ref[...]


def _modulation(c_rows, w_mod, b_mod):
    return pl.pallas_call(
        _mod_kernel,
        out_shape=jax.ShapeDtypeStruct((DEPTH, N_MOD, MOD_ROWS, D_MODEL), F32),
        grid=(DEPTH, N_MOD),
        in_specs=[
            pl.BlockSpec((MOD_ROWS, D_MODEL), lambda l, i: (0, 0)),
            pl.BlockSpec((None, D_MODEL, D_MODEL), lambda l, i: (l, 0, i)),
            pl.BlockSpec((None, 1, D_MODEL), lambda l, i: (l, 0, i)),
        ],
        out_specs=pl.BlockSpec((None, None, MOD_ROWS, D_MODEL), lambda l, i: (l, i, 0, 0)),
        compiler_params=_cparams(("arbitrary", "arbitrary")),
        name="modulation",
    )(c_rows, w_mod, b_mod.reshape(DEPTH, 1, N_MOD * D_MODEL))


def _mod_spec(kind):
    return pl.BlockSpec((None, None, 1, D_MODEL), lambda b, t: (kind, b, 0, 0))


FFN_TM = 512
FFN_CHUNKS = ((0, 1024), (1024, 2048), (2048, 2816))


FFN_ROW_SPLIT = 2
FFN_STAGE_ROWS = 256


FFN_STAGE_SLOTS = 4


def _stream_cast(src, dst, stage, sem):
    rows = src.shape[0]
    nblk = rows // FFN_STAGE_ROWS
    ahead = FFN_STAGE_SLOTS - 1

    def copy(i):
        slot = i % FFN_STAGE_SLOTS
        return pltpu.make_async_copy(src.at[pl.ds(i * FFN_STAGE_ROWS, FFN_STAGE_ROWS), :], stage.at[slot],
                                     sem.at[slot])

    for i in range(min(ahead, nblk)):
        copy(i).start()
    for i in range(nblk):
        if i + ahead < nblk:
            copy(i + ahead).start()
        copy(i).wait()
        dst[i * FFN_STAGE_ROWS:(i + 1) * FFN_STAGE_ROWS, :] = stage[i % FFN_STAGE_SLOTS].astype(BF16)


def _ffn_kernel(*refs, final, nlat_split, layer):
    refs = list(refs)
    x_ref = refs.pop(0)
    xc_ref = refs.pop(0) if nlat_split is not None else None
    sh_ref, sc_ref, gt_ref, g_ref, wg_hbm, wu_hbm, wd_hbm = refs[:7]
    fg_ref = refs[7] if final else None
    o_ref, wg_ref, wu_ref, wd_ref, stage_in, stage_out, sem = refs[-7:]

    @pl.when((pl.program_id(0) == 0) & (pl.program_id(1) == 0))
    def _():
        _stream_cast(wg_hbm.at[layer], wg_ref, stage_in, sem)
        _stream_cast(wu_hbm.at[layer], wu_ref, stage_in, sem)
        _stream_cast(wd_hbm.at[layer], wd_ref, stage_out, sem)

    rb = x_ref.shape[0] // FFN_ROW_SPLIT

    def load_block(r):
        xr = x_ref[r * rb:(r + 1) * rb, :]
        if nlat_split is not None:
            xr = jnp.where(pl.program_id(0) == nlat_split, xc_ref[r * rb:(r + 1) * rb, :], xr)
        return xr

    xs, xn, acc = [None] * FFN_ROW_SPLIT, [None] * FFN_ROW_SPLIT, [None] * FFN_ROW_SPLIT
    for ci, (c0, c1) in enumerate(FFN_CHUNKS):
        for r in range(FFN_ROW_SPLIT):
            if ci == 0:
                xs[r] = load_block(r)
                xn[r] = _ada(xs[r], g_ref[...], sh_ref[...], sc_ref[...]).astype(BF16)
            gate = _dot(xn[r], wg_ref[:, c0:c1])
            up = _dot(xn[r], wu_ref[:, c0:c1])
            h = (_silu(gate) * up).astype(BF16)
            part = _dot(h, wd_ref[c0:c1, :])
            acc[r] = part if acc[r] is None else acc[r] + part
    for r in range(FFN_ROW_SPLIT):
        y = xs[r] + (0.5 * gt_ref[...]) * acc[r]
        if final:
            y = _rms(y, fg_ref[...])
        o_ref[r * rb:(r + 1) * rb, :] = y


def _ffn(x_all, mod_l, kinds, gain, wg, wu, wd, layer, nb, final_gain=None, x_ctx=None):
    seq = x_all.shape[1]
    hbm = pl.BlockSpec(memory_space=pl.ANY)
    final = final_gain is not None
    tok = pl.BlockSpec((None, FFN_TM, D_MODEL), lambda b, t: (b, t, 0))
    if x_ctx is None:
        nlat_split = None
        in_specs, args = [tok], [x_all]
    else:
        nlat_split = x_all.shape[0]
        in_specs = [pl.BlockSpec((None, FFN_TM, D_MODEL), lambda b, t: (jnp.minimum(b, nlat_split - 1), t, 0)),
                    pl.BlockSpec((None, FFN_TM, D_MODEL), lambda b, t: (0, t, 0))]
        args = [x_all, x_ctx]
    in_specs += [
        _mod_spec(kinds[0]), _mod_spec(kinds[1]), _mod_spec(kinds[2]),
        _const_spec((1, D_MODEL)),
        hbm, hbm, hbm,
    ]
    args += [mod_l, mod_l, mod_l, gain.reshape(1, D_MODEL), wg, wu, wd]
    if final:
        in_specs.append(_const_spec((1, D_MODEL)))
        args.append(final_gain.reshape(1, D_MODEL))
    return pl.pallas_call(
        functools.partial(_ffn_kernel, final=final, nlat_split=nlat_split, layer=layer),
        out_shape=jax.ShapeDtypeStruct((nb, seq, D_MODEL), F32),
        grid=(nb, seq // FFN_TM),
        in_specs=in_specs,
        out_specs=pl.BlockSpec((None, FFN_TM, D_MODEL), lambda b, t: (b, t, 0)),
        scratch_shapes=[
            pltpu.VMEM((D_MODEL, D_FF), BF16), pltpu.VMEM((D_MODEL, D_FF), BF16), pltpu.VMEM((D_FF, D_MODEL), BF16),
            pltpu.VMEM((FFN_STAGE_SLOTS, FFN_STAGE_ROWS, D_FF), F32),
            pltpu.VMEM((FFN_STAGE_SLOTS, FFN_STAGE_ROWS, D_MODEL), F32),
            pltpu.SemaphoreType.DMA((FFN_STAGE_SLOTS,)),
        ],
        compiler_params=_cparams(("arbitrary", "arbitrary")),
        name="ffn",
    )(*args)


PA, PB, PC, PZ, PX, PDT = 0, 512, 1280, 1792, 2048, 2816
PROJ_N = PDT + DT_PAD
PROJ_TM = 512
Q_HEAD_ORDER = (0, 2, 1, 3)


def _lane_iota(shape):
    return lax.broadcasted_iota(jnp.int32, shape, len(shape) - 1)


def _rope(t, cos, sin):
    w = t.shape[-1]
    first = (_lane_iota((1, w)) % 32) < 16
    rot = jnp.where(first, pltpu.roll(t, w - 16, 1), pltpu.roll(t, 16, 1))
    return t * cos[:, :w] + rot * sin[:, :w]


def _head_norm(t, gain):
    w = t.shape[-1]
    lane = _lane_iota((1, w))
    sq = t * t
    scale = jnp.zeros_like(t)
    for h in range(w // HEAD_DIM):
        m = (lane >= h * HEAD_DIM) & (lane < (h + 1) * HEAD_DIM)
        ms = jnp.sum(jnp.where(m, sq, 0.0), axis=-1, keepdims=True) * (1.0 / HEAD_DIM)
        scale = jnp.where(m, lax.rsqrt(ms + EPS), scale)
    return (t * scale) * gain[:, :w]


def _pair_heads(t):
    lo = _lane_iota((1, LANE)) < HEAD_DIM
    t0, t1 = t[:, 0:LANE], t[:, LANE:2 * LANE]
    return jnp.concatenate([jnp.where(lo, t0, pltpu.roll(t1, HEAD_DIM, 1)),
                            jnp.where(lo, pltpu.roll(t0, HEAD_DIM, 1), t1)], axis=1)


def _proj_kernel(x_ref, sh_ref, sc_ref, g_ref, w_ref, rope_ref, nq_ref, nk_ref,
                 qa_ref, qb_ref, qc_ref, z_ref, xbc_ref, dt_ref):
    xn = _ada(x_ref[...], g_ref[...], sh_ref[...], sc_ref[...]).astype(BF16)
    cos = rope_ref[:, 0:256]
    sin = rope_ref[:, 256:512]
    a = _dot(xn, w_ref[:, PA:PA + 512])
    qa_ref[:, 0:256] = _pair_heads(_rope(a[:, 0:256], cos, sin) * QK_SCALE).astype(BF16)
    qa_ref[:, 256:384] = _rope(a[:, 256:384], cos, sin).astype(BF16)
    qa_ref[:, 384:512] = a[:, 384:512].astype(BF16)
    b = _dot(xn, w_ref[:, PB:PB + 768])
    qb_ref[:, 0:256] = (b[:, 0:256] * QK_SCALE).astype(BF16)
    qb_ref[:, 256:768] = b[:, 256:768].astype(BF16)
    c = _dot(xn, w_ref[:, PC:PC + 512])
    qc_ref[:, 0:256] = _pair_heads(_rope(_head_norm(c[:, 0:256], nq_ref[...]), cos, sin) * QK_SCALE).astype(BF16)
    qc_ref[:, 256:384] = _rope(_head_norm(c[:, 256:384], nk_ref[...]), cos, sin).astype(BF16)
    qc_ref[:, 384:512] = c[:, 384:512].astype(BF16)
    z_ref[...] = _dot(xn, w_ref[:, PZ:PZ + 256])
    xbc_ref[...] = _dot(xn, w_ref[:, PX:PX + 768])
    dt_ref[...] = _dot(xn, w_ref[:, PDT:PDT + DT_PAD])


def _proj(x_all, mod_l, gain, w_proj, layer, rope_tab, nq, nk, nlat):
    nb, seq, _ = x_all.shape
    tok = lambda w: pl.BlockSpec((None, PROJ_TM, w), lambda b, t: (b, t, 0))
    outs = [(512, BF16), (768, BF16), (512, BF16), (256, F32), (768, F32), (DT_PAD, F32)]
    return pl.pallas_call(
        _proj_kernel,
        out_shape=[jax.ShapeDtypeStruct((nb, seq, w), dt) for w, dt in outs],
        grid=(nb, seq // PROJ_TM),
        in_specs=[
            tok(D_MODEL), _mod_spec(3), _mod_spec(4),
            _const_spec((1, D_MODEL)),
            _layer_spec((D_MODEL, PROJ_N), layer),
            pl.BlockSpec((None, PROJ_TM, 512), lambda b, t: (b // nlat, t, 0)),
            _const_spec((1, 256)), _const_spec((1, 256)),
        ],
        out_specs=[tok(w) for w, _ in outs],
        compiler_params=_cparams(("arbitrary", "arbitrary")),
        name="in_proj",
    )(x_all, mod_l, mod_l, gain.reshape(1, D_MODEL), w_proj, rope_tab, nq, nk)


def _gqa_core(blocks, sink_ref):
    lo = _lane_iota((1, LANE)) < HEAD_DIM
    chain_scores = []
    for q, keys, _, biases in blocks:
        t0, t1 = q[:, 0:LANE], q[:, LANE:2 * LANE]
        zero = jnp.zeros_like(t0)
        for hk in range(2):
            if hk == 0:
                qs = jnp.concatenate([jnp.where(lo, t0, zero), jnp.where(lo, t1, zero)], axis=0)
            else:
                qs = jnp.concatenate([jnp.where(lo, zero, t0), jnp.where(lo, zero, t1)], axis=0)
            scores = []
            for k, bias in zip(keys, biases):
                s = _dot_nt(qs, k)
                scores.append(s if bias is None else s + bias)
            chain_scores.append(scores)
    results = []
    for sb, (q, _, values, _) in enumerate(blocks):
        t = q.shape[0]
        outs = []
        for hk in range(2):
            extra = None
            if sink_ref is not None:
                row = lax.broadcasted_iota(jnp.int32, (2 * t, 1), 0)
                extra = jnp.where(row < t, sink_ref[2 * hk], sink_ref[2 * hk + 1])
            outs.append(_softmax_pv(chain_scores[2 * sb + hk], values, extra, spare_lanes=~lo if hk == 0 else lo))
        o0, o1 = outs
        results.append(jnp.concatenate([jnp.where(lo, o0[0:t], o1[0:t]),
                                        jnp.where(lo, o0[t:2 * t], o1[t:2 * t])], axis=1))
    return results


def _softmax_pv(scores, values, extra_logit=None, spare_lanes=None):
    m = None
    for s in scores:
        sm = jnp.max(s, axis=-1, keepdims=True)
        m = sm if m is None else jnp.maximum(m, sm)
    if extra_logit is not None:
        m = jnp.maximum(m, extra_logit)
    acc = None
    for s, v in zip(scores, values):
        p = jnp.exp((s - m).astype(BF16))
        if spare_lanes is None:
            pv = _dot(p, jnp.concatenate([v, jnp.ones_like(v)], axis=1))
        else:
            pv = _dot(p, jnp.where(spare_lanes, jnp.ones_like(v), v))
        acc = pv if acc is None else acc + pv
    if spare_lanes is None:
        den = acc[:, LANE:2 * LANE]
        acc = acc[:, 0:LANE]
    else:
        den = pltpu.roll(acc, HEAD_DIM, 1)
    if extra_logit is not None:
        den = den + jnp.exp(extra_logit - m)
    return acc * (1.0 / den)


WINDOW_TQ = 256
WINDOW_SUB_BLOCKS = 4
DENSE_TQ = 128
DENSE_SUB_BLOCKS = 4


def _gqa_kernel(*refs, window, seq, tq, sub_blocks):
    if window:
        sink_ref, band_ref, q_ref, k_ref, v_ref, kc_ref, vc_ref, o_ref = refs
    else:
        q_ref, k_ref, v_ref, kc_ref, vc_ref, o_ref = refs
        sink_ref = None
    blocks = []
    for sb in range(sub_blocks):
        q = q_ref[sb * tq:(sb + 1) * tq, :]
        if window:
            nkw = tq + 2 * WA_WINDOW
            q0 = (pl.program_id(1) * sub_blocks + sb) * tq
            start = pl.multiple_of(jnp.clip(q0 - WA_WINDOW, 0, seq - nkw), LANE)
            bias = band_ref[(q0 - start) // WA_WINDOW]
            blocks.append((q, [k_ref[pl.ds(start, nkw), :], kc_ref[...]], [v_ref[pl.ds(start, nkw), :], vc_ref[...]],
                           [bias, None]))
        else:
            blocks.append((q, [k_ref[...], kc_ref[...]], [v_ref[...], vc_ref[...]], [None, None]))
    for sb, o in enumerate(_gqa_core(blocks, sink_ref)):
        o_ref[sb * tq:(sb + 1) * tq, :] = o.astype(BF16)


def _window_band_table(tq):
    off = jnp.arange(3)[:, None, None] * WA_WINDOW
    qpos = (jnp.arange(2 * tq) % tq)[None, :, None] + off
    kpos = jnp.arange(tq + 2 * WA_WINDOW)[None, None, :]
    return jnp.where(jnp.abs(qpos - kpos) <= WA_WINDOW, 0.0, NEG_INF).astype(F32)


def _gqa_attention(qkv, sink, nlat, window):
    nb, seq, _ = qkv.shape
    tq, sub_blocks = (WINDOW_TQ, WINDOW_SUB_BLOCKS) if window else (DENSE_TQ, DENSE_SUB_BLOCKS)
    step = tq * sub_blocks
    in_specs = [
        pl.BlockSpec((None, step, 256), lambda b, i: (b, i, 0)),
        pl.BlockSpec((None, seq, LANE), lambda b, i: (b, 0, 2)),
        pl.BlockSpec((None, seq, LANE), lambda b, i: (b, 0, 3)),
        pl.BlockSpec((None, CTX_LEN, LANE), lambda b, i: (nlat, b, 2)),
        pl.BlockSpec((None, CTX_LEN, LANE), lambda b, i: (nlat, b, 3)),
    ]
    args = [qkv, qkv, qkv, qkv, qkv]
    if window:
        band = _window_band_table(tq)
        in_specs = [pl.BlockSpec(memory_space=pltpu.SMEM), _const_spec(band.shape)] + in_specs
        args = [sink, band] + args
    return pl.pallas_call(
        functools.partial(_gqa_kernel, window=window, seq=seq, tq=tq, sub_blocks=sub_blocks),
        out_shape=jax.ShapeDtypeStruct((nlat, seq, BRANCH_W), BF16),
        grid=(nlat, seq // step),
        in_specs=in_specs,
        out_specs=pl.BlockSpec((None, step, BRANCH_W), lambda b, i: (b, i, 0)),
        compiler_params=_cparams(("arbitrary", "arbitrary")),
        name="window_attention" if window else "dense_attention",
    )(*args)


NA_STEP_ROWS = 4
NA_WIN_ROWS = NA_STEP_ROWS + NA_ROWS
NA_TQ = NA_STEP_ROWS * GRID_W
NA_TK = NA_WIN_ROWS * GRID_W


def _na_window_start(group, rows):
    return jnp.clip(group * NA_STEP_ROWS - NA_ROWS // 2, 0, rows - NA_WIN_ROWS)


NA_STEP_GROUPS = 4


def _na_kernel(q_ref, k_ref, v_ref, kc_ref, vc_ref, bias_ref, o_ref, *, rows):
    lo = _lane_iota((1, LANE)) < HEAD_DIM
    chains = [(gi, blk) for gi in range(NA_STEP_GROUPS) for blk in range(2)]
    starts = [pl.multiple_of(_na_window_start(pl.program_id(0) * NA_STEP_GROUPS + gi, rows) * GRID_W, GRID_W)
              for gi in range(NA_STEP_GROUPS)]
    scores = []
    for gi, blk in chains:
        cols = slice(blk * LANE, (blk + 1) * LANE)
        qb = q_ref[gi * NA_TQ:(gi + 1) * NA_TQ, cols]
        zero = jnp.zeros_like(qb)
        qs = jnp.concatenate([jnp.where(lo, qb, zero), jnp.where(lo, zero, qb)], axis=0)
        s_nb = _dot_nt(qs, k_ref[pl.ds(starts[gi], NA_TK), cols]) + bias_ref[gi, blk]
        scores.append([s_nb, _dot_nt(qs, kc_ref[:, cols])])
    outs = []
    for (gi, blk), sc in zip(chains, scores):
        cols = slice(blk * LANE, (blk + 1) * LANE)
        o = _softmax_pv(sc, [v_ref[pl.ds(starts[gi], NA_TK), cols], vc_ref[:, cols]])
        outs.append(jnp.where(lo, o[0:NA_TQ], o[NA_TQ:2 * NA_TQ]))
    for gi in range(NA_STEP_GROUPS):
        o_ref[gi * NA_TQ:(gi + 1) * NA_TQ, :] = jnp.concatenate(outs[2 * gi:2 * gi + 2], axis=1).astype(BF16)


def _na_attention(qkv, bias, layer, nlat):
    nb, seq, _ = qkv.shape
    rows = seq // GRID_W
    tq = NA_STEP_GROUPS * NA_TQ
    return pl.pallas_call(
        functools.partial(_na_kernel, rows=rows),
        out_shape=jax.ShapeDtypeStruct((nlat, seq, BRANCH_W), BF16),
        grid=(seq // tq, nlat),
        in_specs=[
            pl.BlockSpec((None, tq, 256), lambda g, b: (b, g, 0)),
            pl.BlockSpec((None, seq, 256), lambda g, b: (b, 0, 1)),
            pl.BlockSpec((None, seq, 256), lambda g, b: (b, 0, 2)),
            pl.BlockSpec((None, CTX_LEN, 256), lambda g, b: (nlat, b, 1)),
            pl.BlockSpec((None, CTX_LEN, 256), lambda g, b: (nlat, b, 2)),
            pl.BlockSpec((None, NA_STEP_GROUPS, 2, 2 * NA_TQ, NA_TK), lambda g, b: (layer, g, 0, 0, 0)),
        ],
        out_specs=pl.BlockSpec((None, tq, BRANCH_W), lambda g, b: (b, g, 0)),
        compiler_params=_cparams(("arbitrary", "arbitrary")),
        name="neighborhood_attention",
    )(qkv, qkv, qkv, qkv, qkv, bias)


def _ctx_kernel(sink_ref, qa_ref, ka_ref, va_ref, qb_ref, kb_ref, vb_ref, qc_ref, kc_ref, vc_ref,
                oa_ref, ob_ref, oc_ref):
    t = CTX_LEN
    oa_ref[...] = _gqa_core([(qa_ref[...], [ka_ref[...]], [va_ref[...]], [None])], sink_ref)[0].astype(BF16)
    oc_ref[...] = _gqa_core([(qc_ref[...], [kc_ref[...]], [vc_ref[...]], [None])], None)[0].astype(BF16)
    lo = _lane_iota((1, LANE)) < HEAD_DIM
    outs = []
    for blk in range(2):
        cols = slice(blk * LANE, (blk + 1) * LANE)
        qb = qb_ref[:, cols]
        zero = jnp.zeros_like(qb)
        qs = jnp.concatenate([jnp.where(lo, qb, zero), jnp.where(lo, zero, qb)], axis=0)
        o = _softmax_pv([_dot_nt(qs, kb_ref[:, cols])], [vb_ref[:, cols]])
        outs.append(jnp.where(lo, o[0:t], o[t:2 * t]))
    ob_ref[...] = jnp.concatenate(outs, axis=1).astype(BF16)


def _ctx_attention(sink, qa, qb, qc, nlat):
    blk = lambda w, j: pl.BlockSpec((None, CTX_LEN, w), lambda b: (nlat, b, j))
    out_spec = pl.BlockSpec((None, CTX_LEN, BRANCH_W), lambda b: (0, b, 0))
    return pl.pallas_call(
        _ctx_kernel,
        out_shape=[jax.ShapeDtypeStruct((1, nlat * CTX_LEN, BRANCH_W), BF16)] * 3,
        grid=(nlat,),
        in_specs=[
            pl.BlockSpec(memory_space=pltpu.SMEM),
            blk(256, 0), blk(LANE, 2), blk(LANE, 3),
            blk(256, 0), blk(256, 1), blk(256, 2),
            blk(256, 0), blk(LANE, 2), blk(LANE, 3),
        ],
        out_specs=[out_spec] * 3,
        compiler_params=_cparams(("arbitrary",)),
        name="context_attention",
    )(sink, qa, qa, qa, qb, qb, qb, qc, qc, qc)


CONV_PAD = 8
FIN_ROWS = 256


def _split3(t):
    hi = t.astype(BF16)
    r1 = t - hi.astype(F32)
    mid = r1.astype(BF16)
    return hi, mid, (r1 - mid.astype(F32)).astype(BF16)


LOG2_E = 1.4426950408889634
HEAD_ROWS = 16
CONV_PHASES = 4
N_SLAB = SSM_CONV_CH // LANE


def _ssd_kernel(xbc_ref, xbcc_ref, dt_ref, dtc_ref, z_ref, zc_ref, cw_ref, cb_ref, dtb_ref, alog_ref,
                dvec_ref, nrm_ref, y_ref, yc_ref, pad_ref, u_ref, ct_ref, dt8_ref, w_ref, e_ref,
                yacc_ref, loc_ref, ce_ref, *, seq):
    ltot = CTX_LEN + seq
    nchunk = ltot // SSM_CHUNK
    nctx = CTX_LEN // SSM_CHUNK
    half = SSM_CONV // 2
    cs = SSM_CHUNK
    hr = HEAD_ROWS

    def conv_into(src_ref, n, dst):
        for sl in range(N_SLAB):
            pad_ref[sl, 0:CONV_PAD, :] = jnp.zeros((CONV_PAD, LANE), F32)
            pad_ref[sl, CONV_PAD + n:2 * CONV_PAD + n, :] = jnp.zeros((CONV_PAD, LANE), F32)
            pad_ref[sl, CONV_PAD:CONV_PAD + n, :] = src_ref[:, sl * LANE:(sl + 1) * LANE]
        q = FIN_ROWS // CONV_PHASES

        def block(i, carry):
            r0 = pl.multiple_of(i * FIN_ROWS, FIN_ROWS)
            for sl in range(N_SLAB):
                lanes = slice(sl * LANE, (sl + 1) * LANE)
                for p in range(CONV_PHASES):
                    acc = jnp.broadcast_to(cb_ref[:, lanes], (q, LANE))
                    for k in range(SSM_CONV):
                        tap = pad_ref[sl, pl.ds(r0 + CONV_PAD + p + k - half, q, stride=CONV_PHASES), :]
                        acc = acc + tap * cw_ref[k:k + 1, lanes]
                    u_ref[sl, pl.ds(dst + r0 + p, q, stride=CONV_PHASES), :] = _silu(acc)
            return carry

        lax.fori_loop(0, n // FIN_ROWS, block, 0)

    conv_into(xbcc_ref, CTX_LEN, 0)
    conv_into(xbc_ref, seq, CTX_LEN)

    row = lax.broadcasted_iota(jnp.int32, (cs, cs), 0)
    col = lax.broadcasted_iota(jnp.int32, (cs, cs), 1)
    causal = (col <= row, col >= row)
    lo = _lane_iota((1, LANE)) < SSM_STATE // 2

    def softplus(t):
        return jnp.maximum(t, 0.0) + jnp.log1p(jnp.exp(-jnp.abs(t)))

    sel = jnp.where(lax.broadcasted_iota(jnp.int32, (hr, LANE), 0) == lax.broadcasted_iota(jnp.int32, (hr, LANE), 1),
                    1.0, 0.0).astype(BF16)

    def heads_to_rows(src_ref):
        return sum(_dot_nt(sel, part) for part in _split3(src_ref[...]))

    dt_t = softplus(jnp.concatenate([heads_to_rows(dtc_ref), heads_to_rows(dt_ref)], axis=1) + dtb_ref[...])
    da_t = dt_t * (-jnp.exp(alog_ref[...]))

    def by_chunk(t):
        return jnp.concatenate([t[:, c * cs:(c + 1) * cs] for c in range(nchunk)], axis=0)

    dt_c, da_c = by_chunk(dt_t), by_chunk(da_t)
    tri = jnp.where(row <= col, 1.0, 0.0).astype(BF16)
    pre = sum(_dot(part, tri) for part in _split3(da_c))
    is_fwd = lax.broadcasted_iota(jnp.int32, (nchunk * hr, 1), 0) % hr < SSM_HEADS
    ct = jnp.where(is_fwd, pre, (pre[:, cs - 1:cs] - pre) + da_c) * LOG2_E
    tot = jnp.where(is_fwd, ct[:, cs - 1:cs], ct[:, 0:1])
    ct_ref[...] = ct
    dt8_ref[...] = dt_c
    w_ref[...] = dt_c * jnp.exp2(tot - ct)
    e_ref[...] = jnp.broadcast_to(jnp.exp2(tot), (nchunk * hr, cs))

    def chunk_rows(c):
        return pl.ds(pl.multiple_of(c * cs, cs), cs)

    def head_rows(c):
        return pl.ds(pl.multiple_of(c * hr, hr), hr)

    def phase1(c, carry):
        rows = chunk_rows(c)
        ct8 = ct_ref[head_rows(c), :]
        dt8 = dt8_ref[head_rows(c), :]
        w8 = w_ref[head_rows(c), :]
        for g in range(SSM_GROUPS):
            gl = slice(g * LANE, (g + 1) * LANE)
            xg = u_ref[g, rows, :]
            bg = u_ref[SSM_GROUPS + g, rows, :]
            cg = u_ref[2 * SSM_GROUPS + g, rows, :]
            xcat = jnp.concatenate([jnp.where(lo, xg, 0.0), jnp.where(lo, 0.0, xg)], axis=0).astype(BF16)
            bt = bg.T
            cb = _dot_nt(cg.astype(BF16), bg.astype(BF16))
            mh = [None, None]
            ces, bts = [], []
            for d in range(2):
                for hh in range(2):
                    j = d * SSM_HEADS + 2 * g + hh
                    cum_j = jnp.broadcast_to(ct8[j:j + 1, :], (cs, cs)).T
                    seg = jnp.where(causal[d], cum_j - ct8[j:j + 1, :], NEG_INF)
                    term = jnp.exp2(seg) * dt8[j:j + 1, :]
                    mh[hh] = term if d == 0 else mh[hh] + term
                    ces.append((cg * jnp.exp2(cum_j)).astype(BF16))
                    bts.append((bt * w8[j:j + 1, :]).astype(BF16))
            m = jnp.concatenate([(cb * mh[0]).astype(BF16), (cb * mh[1]).astype(BF16)], axis=1)
            yacc_ref[rows, gl] = _dot(m, xcat)
            ce_ref[rows, g * 4 * LANE:(g + 1) * 4 * LANE] = jnp.concatenate(ces, axis=1)
            for d in range(2):
                st = _dot(jnp.concatenate([bts[2 * d], bts[2 * d + 1]], axis=1), xcat)
                loc_ref[c, :, d * SSM_INNER + g * LANE:d * SSM_INNER + (g + 1) * LANE] = st
        return carry

    lax.fori_loop(0, nchunk, phase1, 0, unroll=2)

    def scan(order, d):
        cols = slice(d * SSM_INNER, (d + 1) * SSM_INNER)

        def body(t, s):
            c = order(t)
            local = loc_ref[c, :, cols]
            loc_ref[c, :, cols] = s
            e8 = e_ref[head_rows(c), :]
            j0 = d * SSM_HEADS
            decay = jnp.concatenate([jnp.where(lo, e8[j0 + 2 * g:j0 + 2 * g + 1, :], e8[j0 + 2 * g + 1:j0 + 2 * g + 2, :])
                                     for g in range(SSM_GROUPS)], axis=1)
            return s * decay + local
        lax.fori_loop(0, nchunk, body, jnp.zeros((SSM_STATE, SSM_INNER), F32))

    scan(lambda t: t, 0)
    scan(lambda t: jnp.where(t < nctx, nctx - 1 - t, nchunk + nctx - 1 - t), 1)

    def finish_chunk(c, z_rows, out_ref, out_rows):
        rows = chunk_rows(c)
        s_in = loc_ref[c]
        ys = []
        for g in range(SSM_GROUPS):
            gl = slice(g * LANE, (g + 1) * LANE)
            sf = s_in[:, gl]
            sb = s_in[:, SSM_INNER + g * LANE:SSM_INNER + (g + 1) * LANE]
            scat = jnp.concatenate([jnp.where(lo, sf, 0.0), jnp.where(lo, 0.0, sf),
                                    jnp.where(lo, sb, 0.0), jnp.where(lo, 0.0, sb)], axis=0).astype(BF16)
            ys.append(yacc_ref[rows, gl] + _dot(ce_ref[rows, g * 4 * LANE:(g + 1) * 4 * LANE], scat))
        xs = jnp.concatenate([u_ref[g, rows, :] for g in range(SSM_GROUPS)], axis=1)
        y = jnp.concatenate(ys, axis=1) + dvec_ref[...] * xs
        y = y * _silu(z_rows)
        out_ref[out_rows, :] = _rms(y, nrm_ref[...]).astype(BF16)

    for c in range(nctx):
        finish_chunk(c, zc_ref[c * cs:(c + 1) * cs, :], yc_ref, slice(c * cs, (c + 1) * cs))

    def finish_body(cl, carry):
        out_rows = chunk_rows(cl)
        finish_chunk(cl + nctx, z_ref[out_rows, :], y_ref, out_rows)
        return carry

    lax.fori_loop(0, nchunk - nctx, finish_body, 0, unroll=2)


def _ssd(xbc, dt, z, conv_w, conv_b, dt_bias, a_log, dvec, nrm, nlat):
    nb, seq, _ = xbc.shape
    ltot = CTX_LEN + seq
    nchunk = ltot // SSM_CHUNK
    lat = lambda w: pl.BlockSpec((None, seq, w), lambda b: (b, 0, 0))
    ctx = lambda w: pl.BlockSpec((None, CTX_LEN, w), lambda b: (nlat, b, 0))
    return pl.pallas_call(
        functools.partial(_ssd_kernel, seq=seq),
        out_shape=[jax.ShapeDtypeStruct((nlat, seq, SSM_INNER), BF16),
                   jax.ShapeDtypeStruct((nlat, CTX_LEN, SSM_INNER), BF16)],
        grid=(nlat,),
        in_specs=[
            lat(SSM_CONV_CH), ctx(SSM_CONV_CH), lat(DT_PAD), ctx(DT_PAD), lat(SSM_INNER), ctx(SSM_INNER),
            _const_spec((8, SSM_CONV_CH)), _const_spec((1, SSM_CONV_CH)),
            _const_spec((HEAD_ROWS, 1)), _const_spec((HEAD_ROWS, 1)),
            _const_spec((1, SSM_INNER)), _const_spec((1, SSM_INNER)),
        ],
        out_specs=[pl.BlockSpec((None, seq, SSM_INNER), lambda b: (b, 0, 0)),
                   pl.BlockSpec((None, CTX_LEN, SSM_INNER), lambda b: (b, 0, 0))],
        scratch_shapes=[
            pltpu.VMEM((N_SLAB, seq + 2 * CONV_PAD, LANE), F32),
            pltpu.VMEM((N_SLAB, ltot, LANE), F32),
            pltpu.VMEM((nchunk * HEAD_ROWS, SSM_CHUNK), F32),
            pltpu.VMEM((nchunk * HEAD_ROWS, SSM_CHUNK), F32),
            pltpu.VMEM((nchunk * HEAD_ROWS, SSM_CHUNK), F32),
            pltpu.VMEM((nchunk * HEAD_ROWS, SSM_CHUNK), F32),
            pltpu.VMEM((ltot, SSM_INNER), F32),
            pltpu.VMEM((nchunk, SSM_STATE, 2 * SSM_INNER), F32),
            pltpu.VMEM((ltot, 2 * SSM_GROUPS * 2 * SSM_STATE), BF16),
        ],
        compiler_params=_cparams(("arbitrary",)),
        name="ssd",
    )(xbc, xbc, dt, dt, z, z, conv_w, conv_b, dt_bias, a_log, dvec, nrm)


MERGE_TM = 512
MERGE_ROW_SPLIT = 2


def _merge_kernel(*refs, nlat_split):
    x_ref, sh_ref, sc_ref, gt_ref, g_ref = refs[:5]
    y_refs = refs[5:9]
    yc_refs = refs[9:13] if nlat_split is not None else None
    wgl_ref, bg_ref, wbr_ref, wo_ref, o_ref = refs[-5:]
    rb = x_ref.shape[0] // MERGE_ROW_SPLIT
    xs, xn, mix = [None] * MERGE_ROW_SPLIT, [None] * MERGE_ROW_SPLIT, [None] * MERGE_ROW_SPLIT
    for k in range(N_BRANCH):
        cols = slice(k * D_MODEL, (k + 1) * D_MODEL)
        for r in range(MERGE_ROW_SPLIT):
            rows = slice(r * rb, (r + 1) * rb)
            if k == 0:
                xs[r] = x_ref[rows, :]
                xn[r] = _ada(xs[r], g_ref[...], sh_ref[...], sc_ref[...]).astype(BF16)
            pre = _dot(xn[r], wgl_ref[:, cols]) + bg_ref[:, cols]
            gate = 1.0 / (1.0 + jnp.exp(-pre))
            y = y_refs[k][rows, :]
            if nlat_split is not None:
                y = jnp.where(pl.program_id(0) == nlat_split, yc_refs[k][rows, :], y)
            term = gate * _dot(y, wbr_ref[k])
            mix[r] = term if mix[r] is None else mix[r] + term
    for r in range(MERGE_ROW_SPLIT):
        o_ref[r * rb:(r + 1) * rb, :] = xs[r] + gt_ref[...] * _dot(mix[r].astype(BF16), wo_ref[...])


def _merge(x_all, mod_l, gain, ys, ys_ctx, w_gate, b_gate, w_branch, w_out, layer, nb):
    seq = x_all.shape[1]
    nlat = ys[0].shape[0]
    tok = lambda w: pl.BlockSpec((None, MERGE_TM, w), lambda b, t: (b, t, 0))
    lat = pl.BlockSpec((None, MERGE_TM, BRANCH_W), lambda b, t: (jnp.minimum(b, nlat - 1), t, 0))
    in_specs = [tok(D_MODEL), _mod_spec(3), _mod_spec(4), _mod_spec(5), _const_spec((1, D_MODEL))] + [lat] * 4
    args = [x_all, mod_l, mod_l, mod_l, gain.reshape(1, D_MODEL), *ys]
    if ys_ctx is not None:
        in_specs += [pl.BlockSpec((None, MERGE_TM, BRANCH_W), lambda b, t: (0, t, 0))] * 4
        args += list(ys_ctx)
    in_specs += [
        _layer_spec((D_MODEL, N_BRANCH * D_MODEL), layer), _layer_spec((1, N_BRANCH * D_MODEL), layer),
        _layer_spec((N_BRANCH, BRANCH_W, D_MODEL), layer), _layer_spec((D_MODEL, D_MODEL), layer),
    ]
    args += [w_gate, b_gate.reshape(DEPTH, 1, N_BRANCH * D_MODEL), w_branch, w_out]
    return pl.pallas_call(
        functools.partial(_merge_kernel, nlat_split=nlat if ys_ctx is not None else None),
        out_shape=jax.ShapeDtypeStruct((nb, seq, D_MODEL), F32),
        grid=(nb, seq // MERGE_TM),
        in_specs=in_specs,
        out_specs=tok(D_MODEL),
        compiler_params=_cparams(("arbitrary", "arbitrary")),
        name="merge",
    )(*args)


def _rope_table(seq):
    pos = jnp.arange(seq)
    axes = jnp.stack([pos // GRID_W, pos % GRID_W], axis=-1).astype(F32)
    quarter = HEAD_DIM // 4
    inv = 1.0 / (ROPE_THETA ** (jnp.arange(quarter, dtype=F32) * 4.0 / HEAD_DIM))
    ang = axes[:, :, None] * inv
    cos, sin = jnp.cos(ang), jnp.sin(ang)
    cos_h = jnp.concatenate([cos, cos], axis=-1).reshape(seq, HEAD_DIM)
    sin_h = jnp.concatenate([-sin, sin], axis=-1).reshape(seq, HEAD_DIM)
    lat = jnp.concatenate([jnp.tile(cos_h, (1, 4)), jnp.tile(sin_h, (1, 4))], axis=-1)
    ident = jnp.concatenate([jnp.ones((seq, 256), F32), jnp.zeros((seq, 256), F32)], axis=-1)
    return jnp.stack([lat, ident])


def _permute_heads(w, axis):
    parts = jnp.split(w, 4, axis=axis)
    return jnp.concatenate([parts[j] for j in Q_HEAD_ORDER], axis=axis)


CAST_SPLIT = 4


def _cast_kernel(w_ref, o_ref):
    o_ref[...] = w_ref[...].astype(BF16)


def _cast_bf16(w):
    depth, r, c = w.shape
    tr = r // CAST_SPLIT
    spec = pl.BlockSpec((None, tr, c), lambda l, i: (l, i, 0))
    return pl.pallas_call(
        _cast_kernel,
        out_shape=jax.ShapeDtypeStruct(w.shape, BF16),
        grid=(depth, CAST_SPLIT),
        in_specs=[spec], out_specs=spec,
        compiler_params=_cparams(("arbitrary", "arbitrary")),
        name="cast_bf16",
    )(w)


MXU_TILE = 256
GATE_COL0 = PDT + 2 * SSM_HEADS


def _transpose_cast_kernel(w_ref, o_ref):
    eye = jnp.where(lax.broadcasted_iota(jnp.int32, (MXU_TILE, MXU_TILE), 0)
                    == lax.broadcasted_iota(jnp.int32, (MXU_TILE, MXU_TILE), 1), 1.0, 0.0).astype(BF16)
    for k0 in range(0, D_MODEL, MXU_TILE):
        blk = w_ref[:, k0:k0 + MXU_TILE].astype(BF16)
        o_ref[k0:k0 + MXU_TILE, :] = _dot_nt(eye, blk).astype(BF16)


def _pack_w_in(w_in_t):
    depth = w_in_t.shape[0]
    proj = pl.pallas_call(
        _transpose_cast_kernel,
        out_shape=jax.ShapeDtypeStruct((depth, D_MODEL, PROJ_N), BF16),
        grid=(depth,),
        in_specs=[pl.BlockSpec((None, PROJ_N, D_MODEL), lambda l: (l, 0, 0))],
        out_specs=pl.BlockSpec((None, D_MODEL, PROJ_N), lambda l: (l, 0, 0)),
        compiler_params=_cparams(("arbitrary",)),
        name="pack_proj",
    )(w_in_t)
    gate = pl.pallas_call(
        lambda w_ref, o_ref: _transpose_cast_kernel(w_ref.at[0], o_ref),
        out_shape=jax.ShapeDtypeStruct((depth, D_MODEL, N_BRANCH * D_MODEL), BF16),
        grid=(depth, N_BRANCH),
        in_specs=[pl.BlockSpec((pl.Element(1), pl.Element(D_MODEL), pl.Element(D_MODEL)),
                               lambda l, k: (l, pl.multiple_of(GATE_COL0 + k * D_MODEL, 8), 0))],
        out_specs=pl.BlockSpec((None, D_MODEL, D_MODEL), lambda l, k: (l, 0, k)),
        compiler_params=_cparams(("arbitrary", "arbitrary")),
        name="pack_gate",
    )(w_in_t)
    return proj, gate


NA_BANDS = 2 * NA_ROWS - 1


def _na_bias_kernel(rpb_ref, o_ref, band_ref, *, rows):
    group = pl.program_id(1)
    c = lax.broadcasted_iota(jnp.int32, (GRID_W, LANE), 0)
    kc = lax.broadcasted_iota(jnp.int32, (GRID_W, LANE), 1)
    lo = kc < GRID_W

    @pl.when(group == 0)
    def _():
        qstart = jnp.clip(c - NA_COLS // 2, 0, GRID_W - NA_COLS)
        ok = (kc >= qstart) & (kc < qstart + NA_COLS)
        for h in range(2 * 2):
            for dr in range(NA_BANDS):
                line = jnp.broadcast_to(rpb_ref[h, dr:dr + 1, :], (GRID_W, LANE))
                skew = pltpu.roll(line, LANE - (NA_COLS - 1), 1, stride=1, stride_axis=0)
                band_ref[h, dr] = jnp.where(ok, skew, NEG_INF)

    w0 = _na_window_start(group, rows)
    for h in range(2 * 2):
        blk, g = divmod(h, 2)
        for qi in range(NA_STEP_ROWS):
            r = group * NA_STEP_ROWS + qi
            rs = jnp.clip(r - NA_ROWS // 2, 0, rows - NA_ROWS)
            for m in range(NA_WIN_ROWS // 2):
                halves = []
                for kr in (w0 + 2 * m, w0 + 2 * m + 1):
                    in_window = (kr >= rs) & (kr < rs + NA_ROWS)
                    dr = jnp.clip(kr - r + NA_ROWS - 1, 0, NA_BANDS - 1)
                    halves.append(jnp.where(in_window, band_ref[h, dr], NEG_INF))
                pair = jnp.where(lo, halves[0], pltpu.roll(halves[1], GRID_W, 1))
                o_ref[blk, g * NA_TQ + qi * GRID_W:g * NA_TQ + (qi + 1) * GRID_W, m * LANE:(m + 1) * LANE] = pair


def _na_bias_table(rpb, rows):
    depth, heads = rpb.shape[:2]
    groups = rows // NA_STEP_ROWS
    padded = jnp.pad(rpb, ((0, 0), (0, 0), (0, 16 - rpb.shape[2]), (0, LANE - rpb.shape[3])))
    return pl.pallas_call(
        functools.partial(_na_bias_kernel, rows=rows),
        out_shape=jax.ShapeDtypeStruct((depth, groups, 2, 2 * NA_TQ, NA_TK), F32),
        grid=(depth, groups),
        in_specs=[pl.BlockSpec((None, heads, 16, LANE), lambda l, g: (l, 0, 0, 0))],
        out_specs=pl.BlockSpec((None, None, 2, 2 * NA_TQ, NA_TK), lambda l, g: (l, g, 0, 0, 0)),
        scratch_shapes=[pltpu.VMEM((heads, 16, GRID_W, LANE), F32)],
        compiler_params=_cparams(("arbitrary", "arbitrary")),
        name="na_bias",
    )(padded)


def kernel(x, c, ctx, c_ctx, w_mod, b_mod, norm_ffn1, ffn1_w_gate, ffn1_w_up, ffn1_w_down, norm_mix, w_in, b_gate,
           attn_sink, na_rpb, qk_norm_q, qk_norm_k, ssm_conv_w, ssm_conv_b, ssm_dt_bias, ssm_a_log, ssm_d, ssm_norm,
           w_branch, w_out, norm_ffn2, ffn2_w_gate, ffn2_w_up, ffn2_w_down, final_norm):
    nlat, seq, _ = x.shape
    assert nlat * CTX_LEN == seq and nlat + 1 <= MOD_ROWS
    nb = nlat + 1
    c_rows = jnp.concatenate([c, c_ctx[None], jnp.zeros((MOD_ROWS - nb, D_MODEL), F32)], axis=0)
    mod = _modulation(c_rows, w_mod, b_mod).reshape(DEPTH, N_MOD, MOD_ROWS, 1, D_MODEL)
    rope_tab = _rope_table(seq)
    head_col = lambda v: jnp.pad(v.reshape(-1, 1), ((0, HEAD_ROWS - v.size), (0, 0)))

    ffn1_w = (ffn1_w_gate, ffn1_w_up, ffn1_w_down)
    ffn2_w = (ffn2_w_gate, ffn2_w_up, ffn2_w_down)
    w_proj, w_gl = _pack_w_in(jnp.swapaxes(w_in, 1, 2))
    w_o = _cast_bf16(w_out)
    wbr = jnp.stack([_permute_heads(w_branch[:, 0], 1), w_branch[:, 1],
                     _permute_heads(w_branch[:, 2], 1), w_branch[:, 3]], axis=1).astype(BF16)
    na_bias = _na_bias_table(na_rpb, seq // GRID_W)

    for l in range(DEPTH):
        with_ctx = l < DEPTH - 1
        last = l == DEPTH - 1
        nq = jnp.tile(qk_norm_q[l], 4).reshape(1, 256)
        nk = jnp.tile(qk_norm_k[l], 4).reshape(1, 256)
        conv_w = jnp.pad(ssm_conv_w[l], ((0, 8 - SSM_CONV), (0, 0)))
        dvec = jnp.repeat(ssm_d[l], SSM_INNER // SSM_HEADS).reshape(1, SSM_INNER)

        if l == 0:
            x_all = _ffn(x, mod[l], (0, 1, 2), norm_ffn1[l], *ffn1_w, l, nb, x_ctx=ctx.reshape(1, seq, D_MODEL))
        else:
            x_all = _ffn(x_all, mod[l], (0, 1, 2), norm_ffn1[l], *ffn1_w, l, nb)
        qa, qb, qc, z, xbc, dt = _proj(x_all, mod[l], norm_mix[l], w_proj, l, rope_tab, nq, nk, nlat)
        ya = _gqa_attention(qa, attn_sink[l], nlat, window=True)
        yb = _na_attention(qb, na_bias, l, nlat)
        yc = _gqa_attention(qc, None, nlat, window=False)
        yd, yd_ctx = _ssd(xbc, dt, z, conv_w, ssm_conv_b[l].reshape(1, -1), head_col(ssm_dt_bias[l]),
                          head_col(ssm_a_log[l]), dvec, ssm_norm[l].reshape(1, -1), nlat)
        nbm = nb if with_ctx else nlat
        ys_ctx = None
        if with_ctx:
            ys_ctx = (*_ctx_attention(attn_sink[l], qa, qb, qc, nlat), yd_ctx.reshape(1, seq, SSM_INNER))
        x_all = _merge(x_all, mod[l], norm_mix[l], (ya, yb, yc, yd), ys_ctx, w_gl, b_gate, wbr, w_o, l, nbm)
        x_all = _ffn(x_all, mod[l], (6, 7, 8), norm_ffn2[l], *ffn2_w, l, nbm,
                     final_gain=final_norm if last else None)
    return x_all
```
